```python
import math
import jax, jax.numpy as jnp
from jax import lax
import numpy as np

D_MODEL = 2048
BATCH = 4
SEQ = 4096
DEPTH = 4

GRID_W = 64
CTX_LEN = 256
EPS = 1e-6
NA_HEADS = 16
NA_HEAD_DIM = 128
NA_WIDTH = NA_HEADS * NA_HEAD_DIM
NA_WIN_H = 8
NA_WIN_W = 16
SSD_EXPAND = 2
SSD_D_INNER = SSD_EXPAND * D_MODEL
SSD_HEAD_DIM = 64
SSD_HEADS = SSD_D_INNER // SSD_HEAD_DIM
SSD_GROUPS = 8
SSD_D_STATE = 128
SSD_CONV = 5
SSD_CHUNK = 128
SSD_BC_WIDTH = SSD_GROUPS * SSD_D_STATE
SSD_CONV_DIM = SSD_D_INNER + 2 * SSD_BC_WIDTH
D_FF = (8 * D_MODEL + 3 * 256 - 1) // (3 * 256) * 256
IN_SPLITS = (NA_WIDTH, NA_WIDTH, NA_WIDTH, SSD_D_INNER, SSD_CONV_DIM, 2 * SSD_HEADS, D_MODEL, D_MODEL)
IN_PROJ_DIM = sum(IN_SPLITS)

kernel_name = 'hybrid_na_ssd_dit_block'


def _split_cols(t, sizes):
    idx = np.cumsum(np.array(sizes))[:-1].tolist()
    return jnp.split(t, idx, axis=-1)


def _rmsnorm(x, w):
    xf = x.astype(jnp.float32)
    y = xf * lax.rsqrt(jnp.mean(xf * xf, axis=-1, keepdims=True) + EPS)
    return (y * w.astype(jnp.float32)).astype(x.dtype)


def _modulate(h, shift, scale):
    return h * (1 + scale) + shift


def _neighbourhood_attention(q, k, v, k_ctx, v_ctx, rpb):
    b, s, h, dh = q.shape
    rows = s // GRID_W
    kh = min(NA_WIN_H, rows)
    scale = dh ** -0.5
    qg = (q * scale).reshape(b, rows, GRID_W, h, dh)
    kg = k.reshape(b, rows, GRID_W, h, dh)
    vg = v.reshape(b, rows, GRID_W, h, dh)
    cols = jnp.arange(GRID_W)
    col_start = jnp.clip(cols - NA_WIN_W // 2, 0, GRID_W - NA_WIN_W)
    col_in = (cols[None, :] >= col_start[:, None]) & (cols[None, :] < col_start[:, None] + NA_WIN_W)
    dx_idx = jnp.clip(cols[None, :] - cols[:, None], -(NA_WIN_W - 1), NA_WIN_W - 1) + NA_WIN_W - 1
    k_ctx_s = k_ctx

    def row_block(r):
        r0 = jnp.clip(r - kh // 2, 0, rows - kh)
        q_r = lax.dynamic_index_in_dim(qg, r, axis=1, keepdims=False)
        k_r = lax.dynamic_slice_in_dim(kg, r0, kh, axis=1)
        v_r = lax.dynamic_slice_in_dim(vg, r0, kh, axis=1)
        dy_idx = r0 + jnp.arange(kh) - r + NA_WIN_H - 1
        bias = rpb[:, dy_idx[None, :, None], dx_idx[:, None, :]]
        s_loc = jnp.einsum('bqhd,brkhd->bhqrk', q_r, k_r).astype(jnp.float32) + bias.astype(jnp.float32)
        s_loc = jnp.where(col_in[None, None, :, None, :], s_loc, -jnp.inf)
        s_loc = s_loc.reshape(b, h, GRID_W, kh * GRID_W)
        s_ctx = jnp.einsum('bqhd,bchd->bhqc', q_r, k_ctx_s).astype(jnp.float32)
        p = jax.nn.softmax(jnp.concatenate([s_loc, s_ctx], axis=-1), axis=-1).astype(v.dtype)
        p_loc = p[..., :kh * GRID_W].reshape(b, h, GRID_W, kh, GRID_W)
        p_ctx = p[..., kh * GRID_W:]
        return (jnp.einsum('bhqrk,brkhd->bqhd', p_loc, v_r)
                + jnp.einsum('bhqc,bchd->bqhd', p_ctx, v_ctx))

    out = lax.map(row_block, jnp.arange(rows))
    return jnp.moveaxis(out, 0, 1).reshape(b, s, h, dh)


def _context_attention(q, k, v):
    scale = q.shape[-1] ** -0.5
    s = jnp.einsum('bqhd,bkhd->bhqk', q * scale, k).astype(jnp.float32)
    p = jax.nn.softmax(s, axis=-1).astype(v.dtype)
    return jnp.einsum('bhqk,bkhd->bqhd', p, v)


def _dwconv_centred(x, w, bias):
    k, ch = w.shape
    pad = k // 2
    y = lax.conv_general_dilated(x, w[:, None, :].astype(x.dtype), window_strides=(1,), padding=[(pad, pad)],
                                 dimension_numbers=('NWC', 'WIO', 'NWC'), feature_group_count=ch)
    return y + bias


def _segsum(a):
    t = a.shape[-1]
    cs = jnp.cumsum(a, axis=-1)
    seg = cs[..., :, None] - cs[..., None, :]
    return jnp.where(jnp.tril(jnp.ones((t, t), dtype=bool)), seg, -jnp.inf)


def _ssd_chunked(xs, dt, a, bm, cm, init_state):
    b, l, h, p = xs.shape
    g, n = bm.shape[2], bm.shape[3]
    e = h // g
    nc = l // SSD_CHUNK
    t = SSD_CHUNK
    xdt = (xs.astype(jnp.float32) * dt[..., None]).reshape(b, nc, t, g, e, p)
    adt = jnp.transpose((dt * a.astype(jnp.float32)).reshape(b, nc, t, g, e), (0, 3, 4, 1, 2))
    a_cs = jnp.cumsum(adt, axis=-1)
    bc = bm.astype(jnp.float32).reshape(b, nc, t, g, n)
    cc = cm.astype(jnp.float32).reshape(b, nc, t, g, n)
    decay_in = jnp.exp(_segsum(adt))
    cb = jnp.einsum('bclgn,bcsgn->bgcls', cc, bc)
    y_diag = jnp.einsum('bgcls,bgecls,bcsgep->bclgep', cb, decay_in, xdt)
    decay_states = jnp.exp(a_cs[..., -1:] - a_cs)
    states = jnp.einsum('bclgn,bgecl,bclgep->bcgepn', bc, decay_states, xdt)
    states = jnp.concatenate([init_state.astype(jnp.float32).reshape(b, 1, g, e, p, n), states], axis=1)
    chunk_tot = jnp.pad(a_cs[..., -1], ((0, 0), (0, 0), (0, 0), (1, 0)))
    decay_chunk = jnp.exp(_segsum(chunk_tot))
    new_states = jnp.einsum('bgezc,bcgepn->bzgepn', decay_chunk, states)
    states, final = new_states[:, :-1], new_states[:, -1]
    y_off = jnp.einsum('bclgn,bcgepn,bgecl->bclgep', cc, states, jnp.exp(a_cs))
    y = (y_diag + y_off).reshape(b, l, h, p)
    return y, final.reshape(b, h, p, n)


def _gated_rmsnorm(y, z, w):
    g = y * jax.nn.silu(z.astype(jnp.float32))
    b, l, d = g.shape
    g = g.reshape(b, l, SSD_GROUPS, d // SSD_GROUPS)
    g = g * lax.rsqrt(jnp.mean(g * g, axis=-1, keepdims=True) + EPS)
    return g.reshape(b, l, d) * w.astype(jnp.float32)


def _bidirectional_ssd(z, xbc, dt_raw, zc, xbcc, dtc_raw, conv_w, conv_b, dt_bias, a_log, d_skip, ssd_norm,
                       with_ctx_out):
    a = -jnp.exp(a_log.astype(jnp.float32))

    def prep(xbc_t, dt_t):
        u = jax.nn.silu(_dwconv_centred(xbc_t, conv_w, conv_b))
        xs, bm, cm = _split_cols(u, (SSD_D_INNER, SSD_BC_WIDTH, SSD_BC_WIDTH))
        bsz, ln = u.shape[0], u.shape[1]
        xs = xs.reshape(bsz, ln, SSD_HEADS, SSD_HEAD_DIM)
        bm = bm.reshape(bsz, ln, SSD_GROUPS, SSD_D_STATE)
        cm = cm.reshape(bsz, ln, SSD_GROUPS, SSD_D_STATE)
        dt = jax.nn.softplus(dt_t.reshape(bsz, ln, 2, SSD_HEADS).astype(jnp.float32) + dt_bias.astype(jnp.float32))
        return xs, bm, cm, dt

    xl, bl, cl, dtl = prep(xbc, dt_raw)
    xc, bc, cc, dtc = prep(xbcc, dtc_raw)
    zero = jnp.zeros((xc.shape[0], SSD_HEADS, SSD_HEAD_DIM, SSD_D_STATE), jnp.float32)
    rev = lambda t: t[:, ::-1]
    yc_f, hc_f = _ssd_chunked(xc, dtc[:, :, 0], a[0], bc, cc, zero)
    yl_f, _ = _ssd_chunked(xl, dtl[:, :, 0], a[0], bl, cl, hc_f)
    yc_b, hc_b = _ssd_chunked(rev(xc), rev(dtc[:, :, 1]), a[1], rev(bc), rev(cc), zero)
    yl_b, _ = _ssd_chunked(rev(xl), rev(dtl[:, :, 1]), a[1], rev(bl), rev(cl), hc_b)

    def finish(y_f, y_b, xs, zt):
        y = y_f + rev(y_b) + d_skip.astype(jnp.float32)[:, None] * xs.astype(jnp.float32)
        y = y.reshape(y.shape[0], y.shape[1], SSD_D_INNER)
        return _gated_rmsnorm(y, zt, ssd_norm).astype(zt.dtype)

    out_l = finish(yl_f, yl_b, xl, z)
    out_c = finish(yc_f, yc_b, xc, zc) if with_ctx_out else None
    return out_l, out_c


def _hybrid_layer(x, xc, mod, mod_c, norm_mix, norm_ffn, w_in, rpb, conv_w, conv_b, dt_bias, a_log, d_skip,
                  ssd_norm, w_br_na, w_br_ssd, w_out, w_gate_up, w_down, with_ctx_out):
    sh_a, sc_a, g_a, sh_f, sc_f, g_f = jnp.split(mod, 6, axis=-1)
    csh_a, csc_a, cg_a, csh_f, csc_f, cg_f = jnp.split(mod_c, 6, axis=-1)
    h = _modulate(_rmsnorm(x, norm_mix), sh_a, sc_a)
    hc = _modulate(_rmsnorm(xc, norm_mix), csh_a, csc_a)
    q, k, v, z, xbc, dt, g_na, g_ssd = _split_cols(h @ w_in, IN_SPLITS)
    qc, kc, vc, zc, xbcc, dtc, g_nac, g_ssdc = _split_cols(hc @ w_in, IN_SPLITS)
    heads = lambda t: t.reshape(t.shape[0], t.shape[1], NA_HEADS, NA_HEAD_DIM)

    attn_l = _neighbourhood_attention(heads(q), heads(k), heads(v), heads(kc), heads(vc), rpb)
    ssd_l, ssd_c = _bidirectional_ssd(z, xbc, dt, zc, xbcc, dtc, conv_w, conv_b, dt_bias, a_log, d_skip,
                                      ssd_norm, with_ctx_out)

    def merge(attn_o, ssd_o, gate1, gate2):
        a_o = attn_o.reshape(attn_o.shape[0], attn_o.shape[1], NA_WIDTH) @ w_br_na
        s_o = ssd_o @ w_br_ssd
        return (jax.nn.sigmoid(gate1) * a_o + jax.nn.sigmoid(gate2) * s_o) @ w_out

    def ffn(t, shift, scale):
        u = _modulate(_rmsnorm(t, norm_ffn), shift, scale)
        gate, up = jnp.split(u @ w_gate_up, 2, axis=-1)
        return (jax.nn.silu(gate) * up) @ w_down

    x = x + g_a * merge(attn_l, ssd_l, g_na, g_ssd)
    x = x + g_f * ffn(x, sh_f, sc_f)
    if with_ctx_out:
        attn_c = _context_attention(heads(qc), heads(kc), heads(vc))
        xc = xc + cg_a * merge(attn_c, ssd_c, g_nac, g_ssdc)
        xc = xc + cg_f * ffn(xc, csh_f, csc_f)
    return x, xc


def setup_inputs(seed: int = 0) -> dict:
    key = jax.random.key(seed)
    ks = jax.random.split(key, 24)
    f32 = jnp.float32
    L = DEPTH
    nrm = lambda k, shape, s: jax.random.normal(k, shape, f32) * s
    dt0 = jnp.exp(jax.random.uniform(ks[12], (L, 2, SSD_HEADS), f32, math.log(1e-3), math.log(1e-1)))
    return {
        'x': nrm(ks[0], (BATCH, SEQ, D_MODEL), 1.0),
        'c': nrm(ks[1], (BATCH, D_MODEL), 1.0),
        'ctx': nrm(ks[2], (BATCH, CTX_LEN, D_MODEL), 1.0),
        'c_ctx': nrm(ks[3], (D_MODEL,), 1.0),
        'w_ada': nrm(ks[4], (L, D_MODEL, 6 * D_MODEL), 0.5 * D_MODEL ** -0.5),
        'b_ada': nrm(ks[5], (L, 6 * D_MODEL), 0.02),
        'norm_mix': 1.0 + nrm(ks[6], (L, D_MODEL), 0.02),
        'norm_ffn': 1.0 + nrm(ks[7], (L, D_MODEL), 0.02),
        'w_in': nrm(ks[8], (L, D_MODEL, IN_PROJ_DIM), D_MODEL ** -0.5),
        'na_rpb': nrm(ks[9], (L, NA_HEADS, 2 * NA_WIN_H - 1, 2 * NA_WIN_W - 1), 0.1),
        'conv_w': nrm(ks[10], (L, SSD_CONV, SSD_CONV_DIM), SSD_CONV ** -0.5),
        'conv_b': nrm(ks[11], (L, SSD_CONV_DIM), 0.02),
        'dt_bias': dt0 + jnp.log(-jnp.expm1(-dt0)),
        'a_log': jnp.log(jax.random.uniform(ks[13], (L, 2, SSD_HEADS), f32, 1.0, 16.0)),
        'd_skip': 1.0 + nrm(ks[14], (L, SSD_HEADS), 0.1),
        'ssd_norm': 1.0 + nrm(ks[15], (L, SSD_D_INNER), 0.02),
        'w_br_na': nrm(ks[16], (L, NA_WIDTH, D_MODEL), NA_WIDTH ** -0.5),
        'w_br_ssd': nrm(ks[17], (L, SSD_D_INNER, D_MODEL), SSD_D_INNER ** -0.5),
        'w_out': nrm(ks[18], (L, D_MODEL, D_MODEL), D_MODEL ** -0.5),
        'w_gate_up': nrm(ks[19], (L, D_MODEL, 2 * D_FF), D_MODEL ** -0.5),
        'w_down': nrm(ks[20], (L, D_FF, D_MODEL), D_FF ** -0.5),
        'norm_final': 1.0 + nrm(ks[21], (D_MODEL,), 0.02),
    }


def reference(x, c, ctx, c_ctx, w_ada, b_ada, norm_mix, norm_ffn, w_in, na_rpb, conv_w, conv_b, dt_bias, a_log,
              d_skip, ssd_norm, w_br_na, w_br_ssd, w_out, w_gate_up, w_down, norm_final):
    xc = ctx
    silu_c = jax.nn.silu(c)
    silu_cc = jax.nn.silu(c_ctx)
    for i in range(DEPTH):
        mod = (silu_c @ w_ada[i] + b_ada[i])[:, None, :]
        mod_c = (silu_cc @ w_ada[i] + b_ada[i])[None, None, :]
        x, xc = _hybrid_layer(x, xc, mod, mod_c, norm_mix[i], norm_ffn[i], w_in[i], na_rpb[i], conv_w[i],
                              conv_b[i], dt_bias[i], a_log[i], d_skip[i], ssd_norm[i], w_br_na[i], w_br_ssd[i],
                              w_out[i], w_gate_up[i], w_down[i], i < DEPTH - 1)
    return _rmsnorm(x, norm_final)
```

```python
import functools

import jax
import jax.numpy as jnp
from jax import lax
from jax.experimental import pallas as pl
from jax.experimental.pallas import tpu as pltpu

F32 = jnp.float32
BF16 = jnp.bfloat16

EPS = 1e-6
GRID_W = 64
NA_HEAD_DIM = 128
NA_WIN_H = 8
NA_WIN_W = 16
SSD_HEAD_DIM = 64
SSD_GROUPS = 8
SSD_D_STATE = 128
SSD_CONV = 5
SSD_CHUNK = 128
MASK_NEG = -1e30

TM = 512
NA_ROWS_PER_STEP = 8
CONV_ROWS = 256
CONV_HALO = 16
VMEM_LIMIT = 48 * 1024 * 1024


def _params(sem):
    return pltpu.CompilerParams(dimension_semantics=sem, vmem_limit_bytes=VMEM_LIMIT)


def _dot(a, b):
    return jnp.dot(a, b, preferred_element_type=F32)


def _dot_nt(a, b):
    return lax.dot_general(a, b, (((1,), (1,)), ((), ())), preferred_element_type=F32)


def _dot_tn(a, b):
    return lax.dot_general(a, b, (((0,), (0,)), ((), ())), preferred_element_type=F32)


def _split3(a):
    p1 = a.astype(BF16)
    r1 = a - p1.astype(F32)
    p2 = r1.astype(BF16)
    r2 = r1 - p2.astype(F32)
    return p1, p2, r2.astype(BF16)


def _sigmoid(v):
    return 1.0 / (1.0 + jnp.exp(-v))


def _silu(v):
    return v * _sigmoid(v)


def _ada_kernel(c_ref, w_ref, b_ref, o_ref):
    s = _silu(c_ref[...])
    w = w_ref[0]
    acc = jnp.zeros(o_ref.shape[1:], F32)
    for sp in _split3(s):
        for wp in _split3(w):
            acc = acc + _dot(sp, wp)
    o_ref[0] = acc + b_ref[0]


def _ada_mod(cvec, w_ada, b_ada, tn=512):
    nl, d, n6 = w_ada.shape
    return pl.pallas_call(
        _ada_kernel,
        name="ada_mod",
        grid=(nl, n6 // tn),
        in_specs=[
            pl.BlockSpec((8, d), lambda l, j: (0, 0)),
            pl.BlockSpec((1, d, tn), lambda l, j: (l, 0, j)),
            pl.BlockSpec((1, 1, tn), lambda l, j: (l, 0, j)),
        ],
        out_specs=pl.BlockSpec((1, 8, tn), lambda l, j: (l, 0, j)),
        out_shape=jax.ShapeDtypeStruct((nl, 8, n6), F32),
        compiler_params=_params(("arbitrary", "arbitrary")),
    )(cvec, w_ada, b_ada.reshape(nl, 1, n6))


def _mod_row(i, n_lat_tiles, tiles_per_batch, ctx_row):
    return jnp.where(i < n_lat_tiles, i // tiles_per_batch, ctx_row)


def _norm_mod(x, nw, sc, sh):
    ms = jnp.mean(x * x, axis=-1, keepdims=True)
    y = x * lax.rsqrt(ms + EPS) * nw
    return y * (1.0 + sc) + sh


def _inproj_kernel(x_ref, nw_ref, sh_ref, sc_ref, w_ref, wdt_ref, o_ref, dt_ref, h_scr, *, q_tiles, q_scale):
    j = pl.program_id(1)

    @pl.when(j == 0)
    def _():
        h = _norm_mod(x_ref[...], nw_ref[...], sc_ref[0], sh_ref[0]).astype(BF16)
        h_scr[...] = h
        dt_ref[...] = _dot(h, wdt_ref[...])

    acc = _dot(h_scr[...], w_ref[...])
    acc = acc * jnp.where(j < q_tiles, q_scale, 1.0)
    o_ref[...] = acc.astype(o_ref.dtype)


def _inproj(xa, nw, mod3, w_main, w_dt, *, n_lat_tiles, tiles_per_batch, ctx_row, na_width, tn=1024):
    r, d = xa.shape
    n = w_main.shape[1]
    ndt = w_dt.shape[1]
    mrow = functools.partial(_mod_row, n_lat_tiles=n_lat_tiles, tiles_per_batch=tiles_per_batch, ctx_row=ctx_row)
    kern = functools.partial(_inproj_kernel, q_tiles=na_width // tn, q_scale=NA_HEAD_DIM ** -0.5)
    return pl.pallas_call(
        kern,
        name="in_proj",
        grid=(r // TM, n // tn),
        in_specs=[
            pl.BlockSpec((TM, d), lambda i, j: (i, 0)),
            pl.BlockSpec((1, d), lambda i, j: (0, 0)),
            pl.BlockSpec((1, 1, d), lambda i, j: (mrow(i), 0, 0)),
            pl.BlockSpec((1, 1, d), lambda i, j: (mrow(i), 0, 1)),
            pl.BlockSpec((d, tn), lambda i, j: (0, j)),
            pl.BlockSpec((d, ndt), lambda i, j: (0, 0)),
        ],
        out_specs=[
            pl.BlockSpec((TM, tn), lambda i, j: (i, j)),
            pl.BlockSpec((TM, ndt), lambda i, j: (i, 0)),
        ],
        out_shape=[jax.ShapeDtypeStruct((r, n), BF16), jax.ShapeDtypeStruct((r, ndt), F32)],
        scratch_shapes=[pltpu.VMEM((TM, d), BF16)],
        compiler_params=_params(("arbitrary", "arbitrary")),
    )(xa, nw.reshape(1, d), mod3, mod3, w_main, w_dt)


def _na_bias_tiles(rpb):
    cols = jnp.arange(GRID_W)
    col_start = jnp.clip(cols - NA_WIN_W // 2, 0, GRID_W - NA_WIN_W)
    col_in = (cols[None, :] >= col_start[:, None]) & (cols[None, :] < col_start[:, None] + NA_WIN_W)
    dx_idx = jnp.clip(cols[None, :] - cols[:, None], -(NA_WIN_W - 1), NA_WIN_W - 1) + NA_WIN_W - 1
    dy = jnp.arange(NA_WIN_H)[:, None] + jnp.arange(NA_WIN_H)[None, :]
    bias = rpb[:, dy[:, :, None, None], dx_idx[None, None, :, :]]
    bias = jnp.where(col_in[None, None, None], bias.astype(F32), MASK_NEG)
    nh = rpb.shape[0]
    return jnp.transpose(bias, (0, 1, 3, 2, 4)).reshape(nh, NA_WIN_H, GRID_W, NA_WIN_H * GRID_W)


def _na_kernel(q_ref, k_ref, v_ref, kc_ref, vc_ref, b_ref, o_ref, *, rows):
    qb = pl.program_id(2)
    kc = kc_ref[...]
    vc = vc_ref[...]
    kh = NA_WIN_H

    def body(rr, carry):
        r = qb * NA_ROWS_PER_STEP + rr
        r0 = jnp.clip(r - kh // 2, 0, rows - kh)
        t = r0 - r + NA_WIN_H - 1
        qoff = pl.multiple_of(rr * GRID_W, GRID_W)
        koff = pl.multiple_of(r0 * GRID_W, GRID_W)
        q = q_ref[pl.ds(qoff, GRID_W), :]
        k = k_ref[pl.ds(koff, kh * GRID_W), :]
        v = v_ref[pl.ds(koff, kh * GRID_W), :]
        s = _dot_nt(q, k) + b_ref[0, t]
        sc = _dot_nt(q, kc)
        m = jnp.maximum(jnp.max(s, axis=-1, keepdims=True), jnp.max(sc, axis=-1, keepdims=True))
        p = jnp.exp(s - m)
        pc = jnp.exp(sc - m)
        l = jnp.sum(p, axis=-1, keepdims=True) + jnp.sum(pc, axis=-1, keepdims=True)
        o = _dot(p.astype(BF16), v) + _dot(pc.astype(BF16), vc)
        o_ref[pl.ds(qoff, GRID_W), :] = (o / l).astype(o_ref.dtype)
        return carry

    lax.fori_loop(0, NA_ROWS_PER_STEP, body, 0)


def _na_attention(p, bias_tiles, *, batch, seq, ctx_len, heads, n_rows_out):
    rows = seq // GRID_W
    assert rows >= NA_WIN_H and rows % NA_ROWS_PER_STEP == 0
    qrows = NA_ROWS_PER_STEP * GRID_W
    nqb = seq // qrows
    ctx_blk0 = (batch * seq) // ctx_len
    dh = NA_HEAD_DIM
    return pl.pallas_call(
        functools.partial(_na_kernel, rows=rows),
        name="na_attn",
        grid=(heads, batch, nqb),
        in_specs=[
            pl.BlockSpec((qrows, dh), lambda h, b, i: (b * nqb + i, h)),
            pl.BlockSpec((seq, dh), lambda h, b, i: (b, heads + h)),
            pl.BlockSpec((seq, dh), lambda h, b, i: (b, 2 * heads + h)),
            pl.BlockSpec((ctx_len, dh), lambda h, b, i: (ctx_blk0 + b, heads + h)),
            pl.BlockSpec((ctx_len, dh), lambda h, b, i: (ctx_blk0 + b, 2 * heads + h)),
            pl.BlockSpec((1, NA_WIN_H, GRID_W, NA_WIN_H * GRID_W), lambda h, b, i: (h, 0, 0, 0)),
        ],
        out_specs=pl.BlockSpec((qrows, dh), lambda h, b, i: (b * nqb + i, h)),
        out_shape=jax.ShapeDtypeStruct((n_rows_out, heads * dh), BF16),
        compiler_params=_params(("arbitrary", "arbitrary", "arbitrary")),
    )(p, p, p, p, p, bias_tiles)


def _ctx_attn_kernel(q_ref, k_ref, v_ref, attn_hbm_ref, o_ref):
    del attn_hbm_ref
    s = _dot_nt(q_ref[...], k_ref[...])
    m = jnp.max(s, axis=-1, keepdims=True)
    e = jnp.exp(s - m)
    l = jnp.sum(e, axis=-1, keepdims=True)
    o_ref[...] = (_dot(e.astype(BF16), v_ref[...]) / l).astype(o_ref.dtype)


def _ctx_attention(p, attn, *, batch, seq, ctx_len, heads):
    ctx_blk0 = (batch * seq) // ctx_len
    dh = NA_HEAD_DIM
    return pl.pallas_call(
        _ctx_attn_kernel,
        name="ctx_attn",
        grid=(batch, heads),
        in_specs=[
            pl.BlockSpec((ctx_len, dh), lambda b, h: (ctx_blk0 + b, h)),
            pl.BlockSpec((ctx_len, dh), lambda b, h: (ctx_blk0 + b, heads + h)),
            pl.BlockSpec((ctx_len, dh), lambda b, h: (ctx_blk0 + b, 2 * heads + h)),
            pl.BlockSpec(memory_space=pl.ANY),
        ],
        out_specs=pl.BlockSpec((ctx_len, dh), lambda b, h: (ctx_blk0 + b, h)),
        out_shape=jax.ShapeDtypeStruct(attn.shape, attn.dtype),
        input_output_aliases={3: 0},
        compiler_params=_params(("arbitrary", "arbitrary")),
    )(p, p, p, attn)


def _conv_kernel(prev_ref, cur_ref, next_ref, w_ref, b_ref, o_ref, ext_ref, *, n_lat_tiles, tiles_per_seq):
    i = pl.program_id(1)
    pos = i % tiles_per_seq
    is_lat = i < n_lat_tiles
    first = jnp.logical_or(jnp.logical_not(is_lat), pos == 0)
    last = jnp.logical_or(jnp.logical_not(is_lat), pos == tiles_per_seq - 1)
    tc = cur_ref.shape[0]
    half = CONV_HALO // 2
    ext_ref[pl.ds(half, tc), :] = cur_ref[...].astype(F32)
    pv = prev_ref[...].astype(F32)
    nv = next_ref[...].astype(F32)
    ext_ref[pl.ds(0, half), :] = jnp.where(first, 0.0, pv[half:, :])
    ext_ref[pl.ds(half + tc, half), :] = jnp.where(last, 0.0, nv[:half, :])
    acc = jnp.broadcast_to(b_ref[...], (tc, cur_ref.shape[1]))
    pad = SSD_CONV // 2
    for kk in range(SSD_CONV):
        acc = acc + w_ref[pl.ds(kk, 1), :] * ext_ref[pl.ds(half - pad + kk, tc), :]
    o_ref[...] = _silu(acc).astype(o_ref.dtype)


def _ssd_conv(p, conv_w, conv_b, *, col0, n_lat_rows, seq, cw=512):
    r = p.shape[0]
    c = conv_w.shape[1]
    tc = CONV_ROWS
    hb = tc // CONV_HALO
    nhalo = r // CONV_HALO
    cb0 = col0 // cw
    assert col0 % cw == 0 and c % cw == 0
    kern = functools.partial(_conv_kernel, n_lat_tiles=n_lat_rows // tc, tiles_per_seq=seq // tc)
    return pl.pallas_call(
        kern,
        name="ssd_conv",
        grid=(c // cw, r // tc),
        in_specs=[
            pl.BlockSpec((CONV_HALO, cw), lambda j, i: (jnp.maximum(i * hb - 1, 0), cb0 + j)),
            pl.BlockSpec((tc, cw), lambda j, i: (i, cb0 + j)),
            pl.BlockSpec((CONV_HALO, cw), lambda j, i: (jnp.minimum((i + 1) * hb, nhalo - 1), cb0 + j)),
            pl.BlockSpec((SSD_CONV, cw), lambda j, i: (0, j)),
            pl.BlockSpec((1, cw), lambda j, i: (0, j)),
        ],
        out_specs=pl.BlockSpec((tc, cw), lambda j, i: (i, j)),
        out_shape=jax.ShapeDtypeStruct((r, c), BF16),
        scratch_shapes=[pltpu.VMEM((tc + CONV_HALO, cw), F32)],
        compiler_params=_params(("arbitrary", "arbitrary")),
    )(p, p, p, conv_w, conv_b.reshape(1, c))


def _ssd_prep_kernel(dtraw_ref, dtb_ref, alog_ref, dt_ref, cs_ref, cst_ref):
    t = dtraw_ref.shape[0]
    nd = dtraw_ref.shape[1]
    v = dtraw_ref[...] + dtb_ref[...]
    dt = jnp.maximum(v, 0.0) + jnp.log1p(jnp.exp(-jnp.abs(v)))
    adt = dt * (-jnp.exp(alog_ref[...]))
    ii = lax.broadcasted_iota(jnp.int32, (t, t), 0)
    jj = lax.broadcasted_iota(jnp.int32, (t, t), 1)
    tri_lo = (ii >= jj).astype(BF16)
    tri_up = (ii <= jj).astype(BF16)
    cs_f = jnp.zeros((t, nd), F32)
    cs_b = jnp.zeros((t, nd), F32)
    for piece in _split3(adt):
        cs_f = cs_f + _dot(tri_lo, piece)
        cs_b = cs_b + _dot(tri_up, piece)
    lane = lax.broadcasted_iota(jnp.int32, (t, nd), 1)
    cs = jnp.where(lane < nd // 2, cs_f, cs_b)
    dt_ref[...] = dt
    cs_ref[...] = cs
    cst_ref[...] = cs.T


def _ssd_prep(dt_raw, dt_bias, a_log):
    r, nd = dt_raw.shape
    t = SSD_CHUNK
    spec = pl.BlockSpec((t, nd), lambda i: (i, 0))
    vec = pl.BlockSpec((1, nd), lambda i: (0, 0))
    out = jax.ShapeDtypeStruct((r, nd), F32)
    return pl.pallas_call(
        _ssd_prep_kernel,
        name="ssd_prep",
        grid=(r // t,),
        in_specs=[spec, vec, vec],
        out_specs=[spec, spec, pl.BlockSpec((nd, t), lambda i: (i, 0))],
        out_shape=[out, out, jax.ShapeDtypeStruct((r // t * nd, t), F32)],
        compiler_params=_params(("arbitrary",)),
    )(dt_raw, dt_bias.reshape(1, nd), a_log.reshape(1, nd))


def _ssd_scan_kernel(*refs, rev, finish, heads_total):
    if finish:
        (x_ref, b_ref, c_ref, dt_ref, cs_ref, cst_ref, yf_ref, z_ref, dskip_ref, nw_ref, o_ref, state_ref) = refs
    else:
        (x_ref, b_ref, c_ref, dt_ref, cs_ref, cst_ref, o_ref, state_ref) = refs
    g = pl.program_id(1)
    s = pl.program_id(2)
    t = SSD_CHUNK
    pd = SSD_HEAD_DIM
    hg = x_ref.shape[1] // pd
    nd = dt_ref.shape[1]
    base = (heads_total if rev else 0) + g * hg

    @pl.when(s == 0)
    def _():
        state_ref[...] = jnp.zeros(state_ref.shape, F32)

    def spread(width):
        rr = lax.broadcasted_iota(jnp.int32, (3 * nd, hg * width), 0) % nd
        cc = lax.broadcasted_iota(jnp.int32, (3 * nd, hg * width), 1) // width
        return (rr == base + cc).astype(BF16)

    def pieces(a):
        return jnp.concatenate(_split3(a), axis=1)

    e_head = spread(pd)
    cs_p = pieces(cs_ref[...])
    dt_e = _dot(pieces(dt_ref[...]), e_head)
    cs_e = _dot(cs_p, e_head)
    cs_l = _dot(cs_p, spread(t))

    xf = x_ref[...].astype(F32)
    xdt = xf * dt_e
    xdt_b = xdt.astype(BF16)
    bm = b_ref[...]
    cm = c_ref[...]
    cb = _dot_nt(cm, bm)

    ii = lax.broadcasted_iota(jnp.int32, (t, t), 0)
    jj = lax.broadcasted_iota(jnp.int32, (t, t), 1)
    tri = (ii <= jj) if rev else (ii >= jj)
    lane = lax.broadcasted_iota(jnp.int32, (t, 2 * pd), 1)
    zero_b = jnp.zeros((t, 2 * pd), BF16)

    state = state_ref[...]
    y_parts = []
    for hp in range(hg // 2):
        xpair = xdt_b[:, hp * 2 * pd:(hp + 1) * 2 * pd]
        acc = None
        for sub in range(2):
            h = hp * 2 + sub
            col = cs_l[:, h * t:(h + 1) * t]
            row = cst_ref[pl.ds(base + h, 1), :]
            decay = jnp.exp(jnp.where(tri, col - row, MASK_NEG))
            mm = (cb * decay).astype(BF16)
            keep = (lane < pd) if sub == 0 else (lane >= pd)
            part = _dot(mm, jnp.where(keep, xpair, zero_b))
            acc = part if acc is None else acc + part
        y_parts.append(acc)
    y = jnp.concatenate(y_parts, axis=1)
    y = y + _dot(cm, state.astype(BF16)) * jnp.exp(cs_e)

    tot = cs_e[0:1, :] if rev else cs_e[t - 1:t, :]
    xds = (xdt * jnp.exp(tot - cs_e)).astype(BF16)
    state_ref[...] = jnp.exp(tot) * state + _dot_tn(bm, xds)

    if finish:
        y = yf_ref[...] + y + dskip_ref[...] * xf
        gz = y * _silu(z_ref[...].astype(F32))
        gz = gz * lax.rsqrt(jnp.mean(gz * gz, axis=-1, keepdims=True) + EPS)
        o_ref[...] = (gz * nw_ref[...]).astype(o_ref.dtype)
    else:
        o_ref[...] = y


def _ssd_scan(u, dt, cs, cst, *, batch, seq, ctx_len, d_inner, rev, finish=None):
    r = u.shape[0]
    t = SSD_CHUNK
    gw = d_inner // SSD_GROUPS
    nd = dt.shape[1]
    lat_chunks = seq // t
    ctx_chunks = ctx_len // t
    steps = lat_chunks + ctx_chunks
    ctx_blk0 = (batch * seq) // t
    xcb = d_inner // gw
    bcb = d_inner // SSD_D_STATE

    def chunk(b, s):
        if rev:
            return jnp.where(s < ctx_chunks, ctx_blk0 + b * ctx_chunks + (ctx_chunks - 1 - s),
                             b * lat_chunks + (steps - 1 - s))
        return jnp.where(s < ctx_chunks, ctx_blk0 + b * ctx_chunks + s, b * lat_chunks + (s - ctx_chunks))

    in_specs = [
        pl.BlockSpec((t, gw), lambda b, g, s: (chunk(b, s), g)),
        pl.BlockSpec((t, SSD_D_STATE), lambda b, g, s: (chunk(b, s), bcb + g)),
        pl.BlockSpec((t, SSD_D_STATE), lambda b, g, s: (chunk(b, s), bcb + SSD_GROUPS + g)),
        pl.BlockSpec((t, nd), lambda b, g, s: (chunk(b, s), 0)),
        pl.BlockSpec((t, nd), lambda b, g, s: (chunk(b, s), 0)),
        pl.BlockSpec((nd, t), lambda b, g, s: (chunk(b, s), 0)),
    ]
    args = [u, u, u, dt, cs, cst]
    if finish is not None:
        y_fwd, p, z_col0, d_skip_e, norm_w = finish
        zcb = z_col0 // gw
        in_specs += [
            pl.BlockSpec((t, gw), lambda b, g, s: (chunk(b, s), g)),
            pl.BlockSpec((t, gw), lambda b, g, s: (chunk(b, s), zcb + g)),
            pl.BlockSpec((1, gw), lambda b, g, s: (0, g)),
            pl.BlockSpec((1, gw), lambda b, g, s: (0, g)),
        ]
        args += [y_fwd, p, d_skip_e, norm_w]
    del xcb
    kern = functools.partial(_ssd_scan_kernel, rev=rev, finish=finish is not None, heads_total=nd // 2)
    return pl.pallas_call(
        kern,
        name="ssd_scan_bwd" if rev else "ssd_scan_fwd",
        grid=(batch, SSD_GROUPS, steps),
        in_specs=in_specs,
        out_specs=pl.BlockSpec((t, gw), lambda b, g, s: (chunk(b, s), g)),
        out_shape=jax.ShapeDtypeStruct((r, d_inner), BF16 if finish is not None else F32),
        scratch_shapes=[pltpu.VMEM((SSD_D_STATE, gw), F32)],
        compiler_params=_params(("arbitrary", "arbitrary", "arbitrary")),
    )(*args)


def _merge_kernel(a_ref, s_ref, wa_ref, ws_ref, g1_ref, g2_ref, o_ref):
    ao = _dot(a_ref[...], wa_ref[...])
    so = _dot(s_ref[...], ws_ref[...])
    o = _sigmoid(g1_ref[...].astype(F32)) * ao + _sigmoid(g2_ref[...].astype(F32)) * so
    o_ref[...] = o.astype(o_ref.dtype)


def _merge(attn, ssd, w_na, w_ssd, p, *, gate_col0, n_rows, tn=512):
    wa = attn.shape[1]
    wsd = ssd.shape[1]
    d = w_na.shape[1]
    gb = gate_col0 // tn
    nj = d // tn
    return pl.pallas_call(
        _merge_kernel,
        name="branch_merge",
        grid=(n_rows // TM, nj),
        in_specs=[
            pl.BlockSpec((TM, wa), lambda i, j: (i, 0)),
            pl.BlockSpec((TM, wsd), lambda i, j: (i, 0)),
            pl.BlockSpec((wa, tn), lambda i, j: (0, j)),
            pl.BlockSpec((wsd, tn), lambda i, j: (0, j)),
            pl.BlockSpec((TM, tn), lambda i, j: (i, gb + j)),
            pl.BlockSpec((TM, tn), lambda i, j: (i, gb + nj + j)),
        ],
        out_specs=pl.BlockSpec((TM, tn), lambda i, j: (i, j)),
        out_shape=jax.ShapeDtypeStruct((n_rows, d), BF16),
        compiler_params=_params(("arbitrary", "arbitrary")),
    )(attn, ssd, w_na, w_ssd, p, p)


def _proj_residual_kernel(m_ref, w_ref, x_ref, g_ref, o_ref):
    o_ref[...] = x_ref[...] + g_ref[0] * _dot(m_ref[...], w_ref[...])


def _proj_residual(m, w, xa, mod3, *, gate_slot, n_rows, n_lat_tiles, tiles_per_batch, ctx_row, tn=512):
    kdim = m.shape[1]
    d = w.shape[1]
    nj = d // tn
    mrow = functools.partial(_mod_row, n_lat_tiles=n_lat_tiles, tiles_per_batch=tiles_per_batch, ctx_row=ctx_row)
    return pl.pallas_call(
        _proj_residual_kernel,
        name="proj_residual",
        grid=(n_rows // TM, nj),
        in_specs=[
            pl.BlockSpec((TM, kdim), lambda i, j: (i, 0)),
            pl.BlockSpec((kdim, tn), lambda i, j: (0, j)),
            pl.BlockSpec((TM, tn), lambda i, j: (i, j)),
            pl.BlockSpec((1, 1, tn), lambda i, j: (mrow(i), 0, gate_slot * nj + j)),
        ],
        out_specs=pl.BlockSpec((TM, tn), lambda i, j: (i, j)),
        out_shape=jax.ShapeDtypeStruct((n_rows, d), F32),
        compiler_params=_params(("arbitrary", "arbitrary")),
    )(m, w, xa, mod3)


def _ffn_up_kernel(x_ref, nw_ref, sh_ref, sc_ref, wg_ref, wu_ref, o_ref, h_scr):
    @pl.when(pl.program_id(1) == 0)
    def _():
        h_scr[...] = _norm_mod(x_ref[...], nw_ref[...], sc_ref[0], sh_ref[0]).astype(BF16)

    h = h_scr[...]
    gate = _dot(h, wg_ref[...])
    up = _dot(h, wu_ref[...])
    o_ref[...] = (_silu(gate) * up).astype(o_ref.dtype)


def _ffn_up(xa, nw, mod3, w_gate_up, *, n_rows, n_lat_tiles, tiles_per_batch, ctx_row, tn=512):
    d = xa.shape[1]
    dff = w_gate_up.shape[1] // 2
    nj = dff // tn
    mrow = functools.partial(_mod_row, n_lat_tiles=n_lat_tiles, tiles_per_batch=tiles_per_batch, ctx_row=ctx_row)
    return pl.pallas_call(
        _ffn_up_kernel,
        name="ffn_up",
        grid=(n_rows // TM, nj),
        in_specs=[
            pl.BlockSpec((TM, d), lambda i, j: (i, 0)),
            pl.BlockSpec((1, d), lambda i, j: (0, 0)),
            pl.BlockSpec((1, 1, d), lambda i, j: (mrow(i), 0, 3)),
            pl.BlockSpec((1, 1, d), lambda i, j: (mrow(i), 0, 4)),
            pl.BlockSpec((d, tn), lambda i, j: (0, j)),
            pl.BlockSpec((d, tn), lambda i, j: (0, nj + j)),
        ],
        out_specs=pl.BlockSpec((TM, tn), lambda i, j: (i, j)),
        out_shape=jax.ShapeDtypeStruct((n_rows, dff), BF16),
        scratch_shapes=[pltpu.VMEM((TM, d), BF16)],
        compiler_params=_params(("arbitrary", "arbitrary")),
    )(xa, nw.reshape(1, d), mod3, mod3, w_gate_up, w_gate_up)


def _final_norm_kernel(x_ref, w_ref, o_ref):
    x = x_ref[...]
    ms = jnp.mean(x * x, axis=-1, keepdims=True)
    o_ref[...] = x * lax.rsqrt(ms + EPS) * w_ref[...]


def _final_norm(xa, w, n_rows):
    d = xa.shape[1]
    return pl.pallas_call(
        _final_norm_kernel,
        name="final_norm",
        grid=(n_rows // TM,),
        in_specs=[pl.BlockSpec((TM, d), lambda i: (i, 0)), pl.BlockSpec((1, d), lambda i: (0, 0))],
        out_specs=pl.BlockSpec((TM, d), lambda i: (i, 0)),
        out_shape=jax.ShapeDtypeStruct((n_rows, d), F32),
        compiler_params=_params(("arbitrary",)),
    )(xa, w.reshape(1, d))


def kernel(x, c, ctx, c_ctx, w_ada, b_ada, norm_mix, norm_ffn, w_in, na_rpb, conv_w, conv_b, dt_bias, a_log,
           d_skip, ssd_norm, w_br_na, w_br_ssd, w_out, w_gate_up, w_down, norm_final):
    batch, seq, d = x.shape
    ctx_len = ctx.shape[1]
    depth = w_ada.shape[0]
    heads = na_rpb.shape[1]
    na_width = heads * NA_HEAD_DIM
    ssd_heads = dt_bias.shape[-1]
    d_inner = ssd_heads * SSD_HEAD_DIM
    conv_dim = conv_w.shape[-1]
    n_lat = batch * seq
    n_all = n_lat + batch * ctx_len
    assert ctx_len == CONV_ROWS and seq % TM == 0 and (batch * ctx_len) % TM == 0 and batch < 8

    tiles = dict(n_lat_tiles=n_lat // TM, tiles_per_batch=seq // TM, ctx_row=batch)

    z_col0 = 3 * na_width
    xbc_col0 = z_col0 + d_inner
    dt_col0 = xbc_col0 + conv_dim
    gate_col0 = dt_col0
    ndt = 2 * ssd_heads

    cvec = jnp.zeros((8, d), F32).at[:batch].set(c).at[batch].set(c_ctx)
    mod = _ada_mod(cvec, w_ada, b_ada)

    xa = jnp.concatenate([x.reshape(n_lat, d), ctx.reshape(batch * ctx_len, d)], axis=0)

    for li in range(depth):
        last = li == depth - 1
        n_rows = n_lat if last else n_all
        mod3 = mod[li].reshape(8, 1, 6 * d)
        w_main = jnp.concatenate([w_in[li][:, :dt_col0], w_in[li][:, dt_col0 + ndt:]], axis=1).astype(BF16)
        w_dt = w_in[li][:, dt_col0:dt_col0 + ndt].astype(BF16)

        p, dt_raw = _inproj(xa, norm_mix[li], mod3, w_main, w_dt, na_width=na_width, **tiles)

        attn = _na_attention(p, _na_bias_tiles(na_rpb[li]), batch=batch, seq=seq, ctx_len=ctx_len, heads=heads,
                             n_rows_out=n_rows)
        if not last:
            attn = _ctx_attention(p, attn, batch=batch, seq=seq, ctx_len=ctx_len, heads=heads)

        u = _ssd_conv(p, conv_w[li], conv_b[li], col0=xbc_col0, n_lat_rows=n_lat, seq=seq)
        dt, cs, cst = _ssd_prep(dt_raw, dt_bias[li], a_log[li])
        scan = functools.partial(_ssd_scan, u, dt, cs, cst, batch=batch, seq=seq, ctx_len=ctx_len, d_inner=d_inner)
        y_fwd = scan(rev=False)
        d_skip_e = jnp.repeat(d_skip[li], SSD_HEAD_DIM).reshape(1, d_inner)
        ssd = scan(rev=True, finish=(y_fwd, p, z_col0, d_skip_e, ssd_norm[li].reshape(1, d_inner)))

        m = _merge(attn, ssd, w_br_na[li].astype(BF16), w_br_ssd[li].astype(BF16), p, gate_col0=gate_col0,
                   n_rows=n_rows)
        xa = _proj_residual(m, w_out[li].astype(BF16), xa, mod3, gate_slot=2, n_rows=n_rows, **tiles)
        hmid = _ffn_up(xa, norm_ffn[li], mod3, w_gate_up[li].astype(BF16), n_rows=n_rows, **tiles)
        xa = _proj_residual(hmid, w_down[li].astype(BF16), xa, mod3, gate_slot=5, n_rows=n_rows, **tiles)

    return _final_norm(xa, norm_final, n_lat).reshape(batch, seq, d)
```

```python
import functools

import jax
import jax.numpy as jnp
from jax import lax
from jax.experimental import pallas as pl
from jax.experimental.pallas import tpu as pltpu

F32 = jnp.float32
BF16 = jnp.bfloat16

EPS = 1e-6
GRID_W = 64
NA_HEAD_DIM = 128
NA_WIN_H = 8
NA_WIN_W = 16
SSD_HEAD_DIM = 64
SSD_GROUPS = 8
SSD_D_STATE = 128
SSD_CONV = 5
SSD_CHUNK = 128
MASK_NEG = -1e30

TM = 512
NA_ROWS_PER_STEP = 16
CONV_ROWS = 256
CONV_HALO = 16
CONV_SUB = 128
SSD_GROUPS_PER_STEP = 4
VMEM_LIMIT = 48 * 1024 * 1024


def _params(sem):
    return pltpu.CompilerParams(dimension_semantics=sem, vmem_limit_bytes=VMEM_LIMIT)


def _dot(a, b):
    return jnp.dot(a, b, preferred_element_type=F32)


def _dot_nt(a, b):
    return lax.dot_general(a, b, (((1,), (1,)), ((), ())), preferred_element_type=F32)


def _dot_tn(a, b):
    return lax.dot_general(a, b, (((0,), (0,)), ((), ())), preferred_element_type=F32)


def _split3(a):
    p1 = a.astype(BF16)
    r1 = a - p1.astype(F32)
    p2 = r1.astype(BF16)
    r2 = r1 - p2.astype(F32)
    return p1, p2, r2.astype(BF16)


def _sigmoid(v):
    return 1.0 / (1.0 + jnp.exp(-v))


def _silu(v):
    return v * _sigmoid(v)


def _ada_kernel(c_ref, w_ref, b_ref, o_ref):
    s = _silu(c_ref[...])
    w = w_ref[0]
    acc = jnp.zeros(o_ref.shape[1:], F32)
    for sp in _split3(s):
        for wp in _split3(w):
            acc = acc + _dot(sp, wp)
    o_ref[0] = acc + b_ref[0]


def _ada_mod(cvec, w_ada, b_ada, tn=512):
    nl, d, n6 = w_ada.shape
    return pl.pallas_call(
        _ada_kernel,
        name="ada_mod",
        grid=(nl, n6 // tn),
        in_specs=[
            pl.BlockSpec((8, d), lambda l, j: (0, 0)),
            pl.BlockSpec((1, d, tn), lambda l, j: (l, 0, j)),
            pl.BlockSpec((1, 1, tn), lambda l, j: (l, 0, j)),
        ],
        out_specs=pl.BlockSpec((1, 8, tn), lambda l, j: (l, 0, j)),
        out_shape=jax.ShapeDtypeStruct((nl, 8, n6), F32),
        compiler_params=_params(("arbitrary", "arbitrary")),
    )(cvec, w_ada, b_ada.reshape(nl, 1, n6))


def _mod_row(i, n_lat_tiles, tiles_per_batch, ctx_row):
    return jnp.where(i < n_lat_tiles, i // tiles_per_batch, ctx_row)


def _norm_mod(x, nw, sc, sh):
    ms = jnp.mean(x * x, axis=-1, keepdims=True)
    y = x * lax.rsqrt(ms + EPS) * nw
    return y * (1.0 + sc) + sh


def _inproj_kernel(x_ref, nw_ref, sh_ref, sc_ref, w_ref, wdt_ref, o_ref, dt_ref, h_scr, *, q_tiles, q_scale):
    j = pl.program_id(1)

    @pl.when(j == 0)
    def _():
        h = _norm_mod(x_ref[...], nw_ref[...], sc_ref[0], sh_ref[0]).astype(BF16)
        h_scr[...] = h
        dt_ref[...] = _dot(h, wdt_ref[...])

    acc = _dot(h_scr[...], w_ref[...])
    acc = acc * jnp.where(j < q_tiles, q_scale, 1.0)
    o_ref[...] = acc.astype(o_ref.dtype)


def _inproj(xa, nw, mod3, w_main, w_dt, *, n_lat_tiles, tiles_per_batch, ctx_row, na_width, tn=1024):
    r, d = xa.shape
    n = w_main.shape[1]
    ndt = w_dt.shape[1]
    mrow = functools.partial(_mod_row, n_lat_tiles=n_lat_tiles, tiles_per_batch=tiles_per_batch, ctx_row=ctx_row)
    kern = functools.partial(_inproj_kernel, q_tiles=na_width // tn, q_scale=NA_HEAD_DIM ** -0.5)
    return pl.pallas_call(
        kern,
        name="in_proj",
        grid=(r // TM, n // tn),
        in_specs=[
            pl.BlockSpec((TM, d), lambda i, j: (i, 0)),
            pl.BlockSpec((1, d), lambda i, j: (0, 0)),
            pl.BlockSpec((1, 1, d), lambda i, j: (mrow(i), 0, 0)),
            pl.BlockSpec((1, 1, d), lambda i, j: (mrow(i), 0, 1)),
            pl.BlockSpec((d, tn), lambda i, j: (0, j)),
            pl.BlockSpec((d, ndt), lambda i, j: (0, 0)),
        ],
        out_specs=[
            pl.BlockSpec((TM, tn), lambda i, j: (i, j)),
            pl.BlockSpec((TM, ndt), lambda i, j: (i, 0)),
        ],
        out_shape=[jax.ShapeDtypeStruct((r, n), BF16), jax.ShapeDtypeStruct((r, ndt), F32)],
        scratch_shapes=[pltpu.VMEM((TM, d), BF16)],
        compiler_params=_params(("arbitrary", "arbitrary")),
    )(xa, nw.reshape(1, d), mod3, mod3, w_main, w_dt)


def _na_bias_tiles(rpb):
    cols = jnp.arange(GRID_W)
    col_start = jnp.clip(cols - NA_WIN_W // 2, 0, GRID_W - NA_WIN_W)
    col_in = (cols[None, :] >= col_start[:, None]) & (cols[None, :] < col_start[:, None] + NA_WIN_W)
    dx_idx = jnp.clip(cols[None, :] - cols[:, None], -(NA_WIN_W - 1), NA_WIN_W - 1) + NA_WIN_W - 1
    onehot = (dx_idx[:, :, None] == jnp.arange(2 * NA_WIN_W - 1)[None, None, :]).astype(F32)
    table = jnp.einsum("hyx,qkx->hyqk", rpb.astype(F32), onehot, precision=lax.Precision.HIGHEST)
    table = jnp.where(col_in[None, None], table, MASK_NEG)
    bias = jnp.stack([table[:, t:t + NA_WIN_H] for t in range(NA_WIN_H)], axis=1)
    nh = rpb.shape[0]
    return jnp.transpose(bias, (0, 1, 3, 2, 4)).reshape(nh, NA_WIN_H, GRID_W, NA_WIN_H * GRID_W)


def _na_kernel(q_ref, k_ref, v_ref, kc_ref, vc_ref, b_ref, o_ref, *, rows):
    qb = pl.program_id(2)
    kc = kc_ref[...]
    vc = vc_ref[...]
    kh = NA_WIN_H

    nr = NA_ROWS_PER_STEP
    koffs, s_loc = [], []
    for rr in range(nr):
        r = qb * nr + rr
        r0 = jnp.clip(r - kh // 2, 0, rows - kh)
        t = r0 - r + NA_WIN_H - 1
        koff = pl.multiple_of(r0 * GRID_W, GRID_W)
        koffs.append(koff)
        q = q_ref[pl.ds(rr * GRID_W, GRID_W), :]
        s_loc.append(_dot_nt(q, k_ref[pl.ds(koff, kh * GRID_W), :]) + b_ref[0, t])
    s_ctx = _dot_nt(q_ref[...], kc)
    m_ctx = jnp.max(s_ctx, axis=-1, keepdims=True)
    m = [jnp.maximum(jnp.max(s_loc[rr], axis=-1, keepdims=True), m_ctx[rr * GRID_W:(rr + 1) * GRID_W])
         for rr in range(nr)]
    p_ctx = jnp.exp(s_ctx - jnp.concatenate(m, axis=0))
    l_ctx = jnp.sum(p_ctx, axis=-1, keepdims=True)
    o_ctx = _dot(p_ctx.astype(BF16), vc)
    p_loc = [jnp.exp(s_loc[rr] - m[rr]) for rr in range(nr)]
    l = [jnp.sum(p_loc[rr], axis=-1, keepdims=True) + l_ctx[rr * GRID_W:(rr + 1) * GRID_W] for rr in range(nr)]
    o_loc = [_dot(p_loc[rr].astype(BF16), v_ref[pl.ds(koffs[rr], kh * GRID_W), :]) for rr in range(nr)]
    for rr in range(nr):
        o = o_loc[rr] + o_ctx[rr * GRID_W:(rr + 1) * GRID_W]
        o_ref[pl.ds(rr * GRID_W, GRID_W), :] = (o / l[rr]).astype(o_ref.dtype)


def _na_attention(p, bias_tiles, *, batch, seq, ctx_len, heads, n_rows_out):
    rows = seq // GRID_W
    assert rows >= NA_WIN_H and rows % NA_ROWS_PER_STEP == 0
    qrows = NA_ROWS_PER_STEP * GRID_W
    nqb = seq // qrows
    ctx_blk0 = (batch * seq) // ctx_len
    dh = NA_HEAD_DIM
    return pl.pallas_call(
        functools.partial(_na_kernel, rows=rows),
        name="na_attn",
        grid=(heads, batch, nqb),
        in_specs=[
            pl.BlockSpec((qrows, dh), lambda h, b, i: (b * nqb + i, h)),
            pl.BlockSpec((seq, dh), lambda h, b, i: (b, heads + h)),
            pl.BlockSpec((seq, dh), lambda h, b, i: (b, 2 * heads + h)),
            pl.BlockSpec((ctx_len, dh), lambda h, b, i: (ctx_blk0 + b, heads + h)),
            pl.BlockSpec((ctx_len, dh), lambda h, b, i: (ctx_blk0 + b, 2 * heads + h)),
            pl.BlockSpec((1, NA_WIN_H, GRID_W, NA_WIN_H * GRID_W), lambda h, b, i: (h, 0, 0, 0)),
        ],
        out_specs=pl.BlockSpec((qrows, dh), lambda h, b, i: (b * nqb + i, h)),
        out_shape=jax.ShapeDtypeStruct((n_rows_out, heads * dh), BF16),
        compiler_params=_params(("arbitrary", "arbitrary", "arbitrary")),
    )(p, p, p, p, p, bias_tiles)


def _ctx_attn_kernel(q_ref, k_ref, v_ref, attn_hbm_ref, o_ref):
    del attn_hbm_ref
    s = _dot_nt(q_ref[...], k_ref[...])
    m = jnp.max(s, axis=-1, keepdims=True)
    e = jnp.exp(s - m)
    l = jnp.sum(e, axis=-1, keepdims=True)
    o_ref[...] = (_dot(e.astype(BF16), v_ref[...]) / l).astype(o_ref.dtype)


def _ctx_attention(p, attn, *, batch, seq, ctx_len, heads):
    ctx_blk0 = (batch * seq) // ctx_len
    dh = NA_HEAD_DIM
    return pl.pallas_call(
        _ctx_attn_kernel,
        name="ctx_attn",
        grid=(batch, heads),
        in_specs=[
            pl.BlockSpec((ctx_len, dh), lambda b, h: (ctx_blk0 + b, h)),
            pl.BlockSpec((ctx_len, dh), lambda b, h: (ctx_blk0 + b, heads + h)),
            pl.BlockSpec((ctx_len, dh), lambda b, h: (ctx_blk0 + b, 2 * heads + h)),
            pl.BlockSpec(memory_space=pl.ANY),
        ],
        out_specs=pl.BlockSpec((ctx_len, dh), lambda b, h: (ctx_blk0 + b, h)),
        out_shape=jax.ShapeDtypeStruct(attn.shape, attn.dtype),
        input_output_aliases={3: 0},
        compiler_params=_params(("arbitrary", "arbitrary")),
    )(p, p, p, attn)


def _conv_kernel(prev_ref, cur_ref, next_ref, w_ref, b_ref, o_ref, *, n_lat_tiles, tiles_per_seq):
    i = pl.program_id(1)
    pos = i % tiles_per_seq
    is_lat = i < n_lat_tiles
    first = jnp.logical_or(jnp.logical_not(is_lat), pos == 0)
    last = jnp.logical_or(jnp.logical_not(is_lat), pos == tiles_per_seq - 1)
    tc = cur_ref.shape[0]
    sub = CONV_SUB
    halo = CONV_HALO
    win = sub + 2 * halo
    pad = SSD_CONV // 2
    taps = [kk for kk in range(SSD_CONV) if kk != pad]
    pv = prev_ref[...]
    nv = next_ref[...]
    ext = jnp.concatenate([jnp.where(first, jnp.zeros_like(pv), pv), cur_ref[...],
                           jnp.where(last, jnp.zeros_like(nv), nv)], axis=0)
    rr = lax.broadcasted_iota(jnp.int32, (len(taps) * sub, win), 0)
    cc = lax.broadcasted_iota(jnp.int32, (len(taps) * sub, win), 1)
    a_idx = rr // sub
    off = jnp.where(a_idx < pad, a_idx, a_idx + 1) - pad
    shift = (cc == halo + rr % sub + off).astype(BF16)
    for blk in range(tc // sub):
        shifted = _dot(shift, ext[blk * sub:blk * sub + win, :])
        acc = b_ref[...] + w_ref[pl.ds(pad, 1), :] * cur_ref[pl.ds(blk * sub, sub), :].astype(F32)
        for a, kk in enumerate(taps):
            acc = acc + w_ref[pl.ds(kk, 1), :] * shifted[a * sub:(a + 1) * sub, :]
        o_ref[pl.ds(blk * sub, sub), :] = _silu(acc).astype(o_ref.dtype)


def _ssd_conv(p, conv_w, conv_b, *, col0, n_lat_rows, seq, cw=512):
    r = p.shape[0]
    c = conv_w.shape[1]
    tc = CONV_ROWS
    hb = tc // CONV_HALO
    nhalo = r // CONV_HALO
    cb0 = col0 // cw
    assert col0 % cw == 0 and c % cw == 0
    kern = functools.partial(_conv_kernel, n_lat_tiles=n_lat_rows // tc, tiles_per_seq=seq // tc)
    return pl.pallas_call(
        kern,
        name="ssd_conv",
        grid=(c // cw, r // tc),
        in_specs=[
            pl.BlockSpec((CONV_HALO, cw), lambda j, i: (jnp.maximum(i * hb - 1, 0), cb0 + j)),
            pl.BlockSpec((tc, cw), lambda j, i: (i, cb0 + j)),
            pl.BlockSpec((CONV_HALO, cw), lambda j, i: (jnp.minimum((i + 1) * hb, nhalo - 1), cb0 + j)),
            pl.BlockSpec((SSD_CONV, cw), lambda j, i: (0, j)),
            pl.BlockSpec((1, cw), lambda j, i: (0, j)),
        ],
        out_specs=pl.BlockSpec((tc, cw), lambda j, i: (i, j)),
        out_shape=jax.ShapeDtypeStruct((r, c), BF16),
        compiler_params=_params(("arbitrary", "arbitrary")),
    )(p, p, p, conv_w, conv_b.reshape(1, c))


def _ssd_prep_kernel(dtraw_ref, dtb_ref, alog_ref, dtt_ref, cst_ref):
    t = dtraw_ref.shape[0]
    nd = dtraw_ref.shape[1]
    v = dtraw_ref[...] + dtb_ref[...]
    dtt = (jnp.maximum(v, 0.0) + jnp.log1p(jnp.exp(-jnp.abs(v)))).T
    adt = dtt * (-jnp.exp(alog_ref[...]))
    jj = lax.broadcasted_iota(jnp.int32, (t, t), 0)
    ii = lax.broadcasted_iota(jnp.int32, (t, t), 1)
    upto = (jj <= ii).astype(BF16)
    from_ = (jj >= ii).astype(BF16)
    cs_f = jnp.zeros((nd, t), F32)
    cs_b = jnp.zeros((nd, t), F32)
    for piece in _split3(adt):
        cs_f = cs_f + _dot(piece, upto)
        cs_b = cs_b + _dot(piece, from_)
    row = lax.broadcasted_iota(jnp.int32, (nd, t), 0)
    dtt_ref[...] = dtt
    cst_ref[...] = jnp.where(row < nd // 2, cs_f, cs_b)


def _ssd_prep(dt_raw, dt_bias, a_log):
    r, nd = dt_raw.shape
    t = SSD_CHUNK
    out = jax.ShapeDtypeStruct((r // t * nd, t), F32)
    ospec = pl.BlockSpec((nd, t), lambda i: (i, 0))
    return pl.pallas_call(
        _ssd_prep_kernel,
        name="ssd_prep",
        grid=(r // t,),
        in_specs=[pl.BlockSpec((t, nd), lambda i: (i, 0)), pl.BlockSpec((1, nd), lambda i: (0, 0)),
                  pl.BlockSpec((nd, 1), lambda i: (0, 0))],
        out_specs=[ospec, ospec],
        out_shape=[out, out],
        compiler_params=_params(("arbitrary",)),
    )(dt_raw, dt_bias.reshape(1, nd), a_log.reshape(nd, 1))


def _ssd_scan_kernel(*refs, rev, finish, ng):
    if finish:
        (x_ref, b_ref, c_ref, dtt_ref, cst_ref, yf_ref, z_ref, dskip_ref, nw_ref, o_ref, state_ref) = refs
    else:
        (x_ref, b_ref, c_ref, dtt_ref, cst_ref, o_ref, state_ref) = refs
    s = pl.program_id(2)
    t = SSD_CHUNK
    pd = SSD_HEAD_DIM
    ns = SSD_D_STATE
    gw = x_ref.shape[1] // ng
    hg = gw // pd

    @pl.when(s == 0)
    def _():
        state_ref[...] = jnp.zeros(state_ref.shape, F32)

    def spread_matrix(width):
        rr = lax.broadcasted_iota(jnp.int32, (4 * hg, hg * width), 0) % hg
        cc = lax.broadcasted_iota(jnp.int32, (4 * hg, hg * width), 1) // width
        return (rr == cc).astype(BF16)

    def spread(a, sel):
        pieces = jnp.concatenate(list(_split3(a)) + [jnp.zeros((hg, t), BF16)], axis=0)
        return _dot_tn(pieces, sel)

    sel_t = spread_matrix(t)
    sel_p = spread_matrix(pd)
    ii = lax.broadcasted_iota(jnp.int32, (t, t), 0)
    jj = lax.broadcasted_iota(jnp.int32, (t, t), 1)
    tri = (ii <= jj) if rev else (ii >= jj)
    lane = lax.broadcasted_iota(jnp.int32, (t, 2 * pd), 1)
    zero_b = jnp.zeros((t, 2 * pd), BF16)
    groups = range(ng)

    dtt = [dtt_ref[pl.ds(gi * hg, hg), :] for gi in groups]
    cst = [cst_ref[pl.ds(gi * hg, hg), :] for gi in groups]
    cs_l = [spread(cst[gi], sel_t) for gi in groups]
    ecs_e = [spread(jnp.exp(cst[gi]), sel_p) for gi in groups]
    w_e = []
    for gi in groups:
        tot_t = cst[gi][:, 0:1] if rev else cst[gi][:, t - 1:t]
        w_e.append(spread(dtt[gi] * jnp.exp(tot_t - cst[gi]), sel_p))
    bm = [b_ref[:, gi * ns:(gi + 1) * ns] for gi in groups]
    cm = [c_ref[:, gi * ns:(gi + 1) * ns] for gi in groups]
    cb = [_dot_nt(cm[gi], bm[gi]) for gi in groups]
    state = [state_ref[gi] for gi in groups]
    y_off = [_dot(cm[gi], state[gi].astype(BF16)) * ecs_e[gi] for gi in groups]

    y = []
    for gi in groups:
        parts = []
        for hp in range(hg // 2):
            xpair = x_ref[:, gi * gw + hp * 2 * pd:gi * gw + (hp + 1) * 2 * pd]
            acc = None
            for sub in range(2):
                h = hp * 2 + sub
                decay = jnp.exp(jnp.where(tri, cs_l[gi][:, h * t:(h + 1) * t] - cst[gi][h:h + 1, :], MASK_NEG))
                mm = (cb[gi] * decay * dtt[gi][h:h + 1, :]).astype(BF16)
                keep = (lane < pd) if sub == 0 else (lane >= pd)
                part = _dot(mm, jnp.where(keep, xpair, zero_b))
                acc = part if acc is None else acc + part
            parts.append(acc)
        y.append(jnp.concatenate(parts, axis=1) + y_off[gi])

    for gi in groups:
        xf = x_ref[:, gi * gw:(gi + 1) * gw].astype(F32)
        etot = ecs_e[gi][0:1, :] if rev else ecs_e[gi][t - 1:t, :]
        state_ref[gi] = etot * state[gi] + _dot_tn(bm[gi], (xf * w_e[gi]).astype(BF16))

    for gi in groups:
        cols = slice(gi * gw, (gi + 1) * gw)
        if finish:
            yt = yf_ref[:, cols] + y[gi] + dskip_ref[:, cols] * x_ref[:, cols].astype(F32)
            gz = yt * _silu(z_ref[:, cols].astype(F32))
            gz = gz * lax.rsqrt(jnp.mean(gz * gz, axis=-1, keepdims=True) + EPS)
            o_ref[:, cols] = (gz * nw_ref[:, cols]).astype(o_ref.dtype)
        else:
            o_ref[:, cols] = y[gi]


def _ssd_scan(u, dtt, cst, *, batch, seq, ctx_len, d_inner, rev, finish=None):
    r = u.shape[0]
    t = SSD_CHUNK
    ng = SSD_GROUPS_PER_STEP
    gw = d_inner // SSD_GROUPS
    hg = gw // SSD_HEAD_DIM
    bw = ng * gw
    sw = ng * SSD_D_STATE
    assert SSD_GROUPS % ng == 0 and d_inner % sw == 0
    dir_blocks = SSD_GROUPS // ng
    dir0 = dir_blocks if rev else 0
    lat_chunks = seq // t
    ctx_chunks = ctx_len // t
    steps = lat_chunks + ctx_chunks
    ctx_blk0 = (batch * seq) // t
    bcb = d_inner // sw

    def chunk(b, s):
        if rev:
            return jnp.where(s < ctx_chunks, ctx_blk0 + b * ctx_chunks + (ctx_chunks - 1 - s),
                             b * lat_chunks + (steps - 1 - s))
        return jnp.where(s < ctx_chunks, ctx_blk0 + b * ctx_chunks + s, b * lat_chunks + (s - ctx_chunks))

    in_specs = [
        pl.BlockSpec((t, bw), lambda b, g, s: (chunk(b, s), g)),
        pl.BlockSpec((t, sw), lambda b, g, s: (chunk(b, s), bcb + g)),
        pl.BlockSpec((t, sw), lambda b, g, s: (chunk(b, s), bcb + dir_blocks + g)),
        pl.BlockSpec((ng * hg, t), lambda b, g, s: (chunk(b, s) * 2 * dir_blocks + dir0 + g, 0)),
        pl.BlockSpec((ng * hg, t), lambda b, g, s: (chunk(b, s) * 2 * dir_blocks + dir0 + g, 0)),
    ]
    args = [u, u, u, dtt, cst]
    if finish is not None:
        y_fwd, p, z_col0, d_skip_e, norm_w = finish
        zcb = z_col0 // bw
        assert z_col0 % bw == 0
        in_specs += [
            pl.BlockSpec((t, bw), lambda b, g, s: (chunk(b, s), g)),
            pl.BlockSpec((t, bw), lambda b, g, s: (chunk(b, s), zcb + g)),
            pl.BlockSpec((1, bw), lambda b, g, s: (0, g)),
            pl.BlockSpec((1, bw), lambda b, g, s: (0, g)),
        ]
        args += [y_fwd, p, d_skip_e, norm_w]
    kern = functools.partial(_ssd_scan_kernel, rev=rev, finish=finish is not None, ng=ng)
    return pl.pallas_call(
        kern,
        name="ssd_scan_bwd" if rev else "ssd_scan_fwd",
        grid=(batch, dir_blocks, steps),
        in_specs=in_specs,
        out_specs=pl.BlockSpec((t, bw), lambda b, g, s: (chunk(b, s), g)),
        out_shape=jax.ShapeDtypeStruct((r, d_inner), BF16 if finish is not None else F32),
        scratch_shapes=[pltpu.VMEM((ng, SSD_D_STATE, gw), F32)],
        compiler_params=_params(("arbitrary", "arbitrary", "arbitrary")),
    )(*args)


def _merge_kernel(a_ref, s_ref, wa_ref, ws_ref, g1_ref, g2_ref, o_ref):
    ao = _dot(a_ref[...], wa_ref[...])
    so = _dot(s_ref[...], ws_ref[...])
    o = _sigmoid(g1_ref[...].astype(F32)) * ao + _sigmoid(g2_ref[...].astype(F32)) * so
    o_ref[...] = o.astype(o_ref.dtype)


def _merge(attn, ssd, w_na, w_ssd, p, *, gate_col0, n_rows, tn=512):
    wa = attn.shape[1]
    wsd = ssd.shape[1]
    d = w_na.shape[1]
    gb = gate_col0 // tn
    nj = d // tn
    return pl.pallas_call(
        _merge_kernel,
        name="branch_merge",
        grid=(n_rows // TM, nj),
        in_specs=[
            pl.BlockSpec((TM, wa), lambda i, j: (i, 0)),
            pl.BlockSpec((TM, wsd), lambda i, j: (i, 0)),
            pl.BlockSpec((wa, tn), lambda i, j: (0, j)),
            pl.BlockSpec((wsd, tn), lambda i, j: (0, j)),
            pl.BlockSpec((TM, tn), lambda i, j: (i, gb + j)),
            pl.BlockSpec((TM, tn), lambda i, j: (i, gb + nj + j)),
        ],
        out_specs=pl.BlockSpec((TM, tn), lambda i, j: (i, j)),
        out_shape=jax.ShapeDtypeStruct((n_rows, d), BF16),
        compiler_params=_params(("arbitrary", "arbitrary")),
    )(attn, ssd, w_na, w_ssd, p, p)


def _proj_residual_kernel(m_ref, w_ref, x_ref, g_ref, o_ref):
    o_ref[...] = x_ref[...] + g_ref[0] * _dot(m_ref[...], w_ref[...])


def _proj_residual(m, w, xa, mod3, *, gate_slot, n_rows, n_lat_tiles, tiles_per_batch, ctx_row, tn=512):
    kdim = m.shape[1]
    d = w.shape[1]
    nj = d // tn
    mrow = functools.partial(_mod_row, n_lat_tiles=n_lat_tiles, tiles_per_batch=tiles_per_batch, ctx_row=ctx_row)
    return pl.pallas_call(
        _proj_residual_kernel,
        name="proj_residual",
        grid=(n_rows // TM, nj),
        in_specs=[
            pl.BlockSpec((TM, kdim), lambda i, j: (i, 0)),
            pl.BlockSpec((kdim, tn), lambda i, j: (0, j)),
            pl.BlockSpec((TM, tn), lambda i, j: (i, j)),
            pl.BlockSpec((1, 1, tn), lambda i, j: (mrow(i), 0, gate_slot * nj + j)),
        ],
        out_specs=pl.BlockSpec((TM, tn), lambda i, j: (i, j)),
        out_shape=jax.ShapeDtypeStruct((n_rows, d), F32),
        compiler_params=_params(("arbitrary", "arbitrary")),
    )(m, w, xa, mod3)


def _ffn_up_kernel(x_ref, nw_ref, sh_ref, sc_ref, wg_ref, wu_ref, o_ref, h_scr):
    @pl.when(pl.program_id(1) == 0)
    def _():
        h_scr[...] = _norm_mod(x_ref[...], nw_ref[...], sc_ref[0], sh_ref[0]).astype(BF16)

    h = h_scr[...]
    gate = _dot(h, wg_ref[...])
    up = _dot(h, wu_ref[...])
    o_ref[...] = (_silu(gate) * up).astype(o_ref.dtype)


def _ffn_up(xa, nw, mod3, w_gate_up, *, n_rows, n_lat_tiles, tiles_per_batch, ctx_row, tn=512):
    d = xa.shape[1]
    dff = w_gate_up.shape[1] // 2
    nj = dff // tn
    mrow = functools.partial(_mod_row, n_lat_tiles=n_lat_tiles, tiles_per_batch=tiles_per_batch, ctx_row=ctx_row)
    return pl.pallas_call(
        _ffn_up_kernel,
        name="ffn_up",
        grid=(n_rows // TM, nj),
        in_specs=[
            pl.BlockSpec((TM, d), lambda i, j: (i, 0)),
            pl.BlockSpec((1, d), lambda i, j: (0, 0)),
            pl.BlockSpec((1, 1, d), lambda i, j: (mrow(i), 0, 3)),
            pl.BlockSpec((1, 1, d), lambda i, j: (mrow(i), 0, 4)),
            pl.BlockSpec((d, tn), lambda i, j: (0, j)),
            pl.BlockSpec((d, tn), lambda i, j: (0, nj + j)),
        ],
        out_specs=pl.BlockSpec((TM, tn), lambda i, j: (i, j)),
        out_shape=jax.ShapeDtypeStruct((n_rows, dff), BF16),
        scratch_shapes=[pltpu.VMEM((TM, d), BF16)],
        compiler_params=_params(("arbitrary", "arbitrary")),
    )(xa, nw.reshape(1, d), mod3, mod3, w_gate_up, w_gate_up)


def _final_norm_kernel(x_ref, w_ref, o_ref):
    x = x_ref[...]
    ms = jnp.mean(x * x, axis=-1, keepdims=True)
    o_ref[...] = x * lax.rsqrt(ms + EPS) * w_ref[...]


def _final_norm(xa, w, n_rows):
    d = xa.shape[1]
    return pl.pallas_call(
        _final_norm_kernel,
        name="final_norm",
        grid=(n_rows // TM,),
        in_specs=[pl.BlockSpec((TM, d), lambda i: (i, 0)), pl.BlockSpec((1, d), lambda i: (0, 0))],
        out_specs=pl.BlockSpec((TM, d), lambda i: (i, 0)),
        out_shape=jax.ShapeDtypeStruct((n_rows, d), F32),
        compiler_params=_params(("arbitrary",)),
    )(xa, w.reshape(1, d))


def kernel(x, c, ctx, c_ctx, w_ada, b_ada, norm_mix, norm_ffn, w_in, na_rpb, conv_w, conv_b, dt_bias, a_log,
           d_skip, ssd_norm, w_br_na, w_br_ssd, w_out, w_gate_up, w_down, norm_final):
    batch, seq, d = x.shape
    ctx_len = ctx.shape[1]
    depth = w_ada.shape[0]
    heads = na_rpb.shape[1]
    na_width = heads * NA_HEAD_DIM
    ssd_heads = dt_bias.shape[-1]
    d_inner = ssd_heads * SSD_HEAD_DIM
    conv_dim = conv_w.shape[-1]
    n_lat = batch * seq
    n_all = n_lat + batch * ctx_len
    assert ctx_len == CONV_ROWS and seq % TM == 0 and (batch * ctx_len) % TM == 0 and batch < 8

    tiles = dict(n_lat_tiles=n_lat // TM, tiles_per_batch=seq // TM, ctx_row=batch)

    z_col0 = 3 * na_width
    xbc_col0 = z_col0 + d_inner
    dt_col0 = xbc_col0 + conv_dim
    gate_col0 = dt_col0
    ndt = 2 * ssd_heads

    cvec = jnp.zeros((8, d), F32).at[:batch].set(c).at[batch].set(c_ctx)
    mod = _ada_mod(cvec, w_ada, b_ada)

    xa = jnp.concatenate([x.reshape(n_lat, d), ctx.reshape(batch * ctx_len, d)], axis=0)

    for li in range(depth):
        last = li == depth - 1
        n_rows = n_lat if last else n_all
        mod3 = mod[li].reshape(8, 1, 6 * d)
        w_main = jnp.concatenate([w_in[li][:, :dt_col0], w_in[li][:, dt_col0 + ndt:]], axis=1).astype(BF16)
        w_dt = w_in[li][:, dt_col0:dt_col0 + ndt].astype(BF16)

        p, dt_raw = _inproj(xa, norm_mix[li], mod3, w_main, w_dt, na_width=na_width, **tiles)

        attn = _na_attention(p, _na_bias_tiles(na_rpb[li]), batch=batch, seq=seq, ctx_len=ctx_len, heads=heads,
                             n_rows_out=n_rows)
        if not last:
            attn = _ctx_attention(p, attn, batch=batch, seq=seq, ctx_len=ctx_len, heads=heads)

        u = _ssd_conv(p, conv_w[li], conv_b[li], col0=xbc_col0, n_lat_rows=n_lat, seq=seq)
        dtt, cst = _ssd_prep(dt_raw, dt_bias[li], a_log[li])
        scan = functools.partial(_ssd_scan, u, dtt, cst, batch=batch, seq=seq, ctx_len=ctx_len, d_inner=d_inner)
        y_fwd = scan(rev=False)
        d_skip_e = jnp.repeat(d_skip[li], SSD_HEAD_DIM).reshape(1, d_inner)
        ssd = scan(rev=True, finish=(y_fwd, p, z_col0, d_skip_e, ssd_norm[li].reshape(1, d_inner)))

        m = _merge(attn, ssd, w_br_na[li].astype(BF16), w_br_ssd[li].astype(BF16), p, gate_col0=gate_col0,
                   n_rows=n_rows)
        xa = _proj_residual(m, w_out[li].astype(BF16), xa, mod3, gate_slot=2, n_rows=n_rows, **tiles)
        hmid = _ffn_up(xa, norm_ffn[li], mod3, w_gate_up[li].astype(BF16), n_rows=n_rows, **tiles)
        xa = _proj_residual(hmid, w_down[li].astype(BF16), xa, mod3, gate_slot=5, n_rows=n_rows, **tiles)

    return _final_norm(xa, norm_final, n_lat).reshape(batch, seq, d)
```

```python
import functools

import jax
import jax.numpy as jnp
from jax import lax
from jax.experimental import pallas as pl
from jax.experimental.pallas import tpu as pltpu

F32 = jnp.float32
BF16 = jnp.bfloat16

EPS = 1e-6
GRID_W = 64
NA_HEAD_DIM = 128
NA_WIN_H = 8
NA_WIN_W = 16
SSD_HEAD_DIM = 64
SSD_GROUPS = 8
SSD_D_STATE = 128
SSD_CONV = 5
SSD_CHUNK = 128
MASK_NEG = -1e30
LOG2E = 1.4426950408889634

TM = 512
TM_BIG = 1024
NORM_SLAB = 256
NA_ROWS_PER_STEP = 16
CONV_ROWS = 256
CONV_HALO = 16
CONV_SUB = 128
SSD_GROUPS_PER_STEP = 4
VMEM_LIMIT = 56 * 1024 * 1024


def _params(sem):
    return pltpu.CompilerParams(dimension_semantics=sem, vmem_limit_bytes=VMEM_LIMIT)


def _dot(a, b):
    return jnp.dot(a, b, preferred_element_type=F32)


def _dot_nt(a, b):
    return lax.dot_general(a, b, (((1,), (1,)), ((), ())), preferred_element_type=F32)


def _dot_tn(a, b):
    return lax.dot_general(a, b, (((0,), (0,)), ((), ())), preferred_element_type=F32)


def _split3(a):
    p1 = a.astype(BF16)
    r1 = a - p1.astype(F32)
    p2 = r1.astype(BF16)
    r2 = r1 - p2.astype(F32)
    return p1, p2, r2.astype(BF16)


def _sigmoid(v):
    return 1.0 / (1.0 + jnp.exp(-v))


def _silu(v):
    return v * _sigmoid(v)


def _ada_kernel(c_ref, w_ref, b_ref, o_ref):
    s = _silu(c_ref[...])
    w = w_ref[0]
    acc = jnp.zeros(o_ref.shape[1:], F32)
    for sp in _split3(s):
        for wp in _split3(w):
            acc = acc + _dot(sp, wp)
    o_ref[0] = acc + b_ref[0]


def _ada_mod(cvec, w_ada, b_ada, tn=512):
    nl, d, n6 = w_ada.shape
    return pl.pallas_call(
        _ada_kernel,
        name="ada_mod",
        grid=(nl, n6 // tn),
        in_specs=[
            pl.BlockSpec((8, d), lambda l, j: (0, 0)),
            pl.BlockSpec((1, d, tn), lambda l, j: (l, 0, j)),
            pl.BlockSpec((1, 1, tn), lambda l, j: (l, 0, j)),
        ],
        out_specs=pl.BlockSpec((1, 8, tn), lambda l, j: (l, 0, j)),
        out_shape=jax.ShapeDtypeStruct((nl, 8, n6), F32),
        compiler_params=_params(("arbitrary", "arbitrary")),
    )(cvec, w_ada, b_ada.reshape(nl, 1, n6))


def _mod_row(i, n_lat_tiles, tiles_per_batch, ctx_row):
    return jnp.where(i < n_lat_tiles, i // tiles_per_batch, ctx_row)


def _norm_mod(x, nw, sc, sh):
    ms = jnp.mean(x * x, axis=-1, keepdims=True)
    y = x * lax.rsqrt(ms + EPS) * nw
    return y * (1.0 + sc) + sh


def _norm_mod_to(h_scr, x_ref, nw_ref, sc_ref, sh_ref):
    nw, sc, sh = nw_ref[...], sc_ref[0], sh_ref[0]

    def slab(k, carry):
        rows = pl.ds(pl.multiple_of(k * NORM_SLAB, NORM_SLAB), NORM_SLAB)
        h_scr[rows, :] = _norm_mod(x_ref[rows, :], nw, sc, sh).astype(BF16)
        return carry

    lax.fori_loop(0, x_ref.shape[0] // NORM_SLAB, slab, 0)


def _col_tiles(w, tn):
    k, n = w.shape
    return jnp.transpose(w.astype(BF16).reshape(k, n // tn, tn), (1, 0, 2))


def _inproj_kernel(x_ref, nw_ref, sh_ref, sc_ref, w_ref, wdt_ref, o_ref, dt_ref, h_scr, *, q_tiles, q_scale):
    j = pl.program_id(1)

    @pl.when(j == 0)
    def _():
        _norm_mod_to(h_scr, x_ref, nw_ref, sc_ref, sh_ref)
        dt_ref[...] = _dot(h_scr[...], wdt_ref[...])

    acc = _dot(h_scr[...], w_ref[0])
    acc = acc * jnp.where(j < q_tiles, q_scale, 1.0)
    o_ref[...] = acc.astype(o_ref.dtype)


def _inproj(xa, nw, mod3, w_tiles, w_dt, *, seq, n_lat, ctx_row, na_width, tm=1024):
    r, d = xa.shape
    nj, _, tn = w_tiles.shape
    ndt = w_dt.shape[1]
    mrow = functools.partial(_mod_row, n_lat_tiles=n_lat // tm, tiles_per_batch=seq // tm, ctx_row=ctx_row)
    kern = functools.partial(_inproj_kernel, q_tiles=na_width // tn, q_scale=NA_HEAD_DIM ** -0.5)
    return pl.pallas_call(
        kern,
        name="in_proj",
        grid=(r // tm, nj),
        in_specs=[
            pl.BlockSpec((tm, d), lambda i, j: (i, 0)),
            pl.BlockSpec((1, d), lambda i, j: (0, 0)),
            pl.BlockSpec((1, 1, d), lambda i, j: (mrow(i), 0, 0)),
            pl.BlockSpec((1, 1, d), lambda i, j: (mrow(i), 0, 1)),
            pl.BlockSpec((1, d, tn), lambda i, j: (j, 0, 0)),
            pl.BlockSpec((d, ndt), lambda i, j: (0, 0)),
        ],
        out_specs=[
            pl.BlockSpec((tm, tn), lambda i, j: (i, j)),
            pl.BlockSpec((tm, ndt), lambda i, j: (i, 0)),
        ],
        out_shape=[jax.ShapeDtypeStruct((r, nj * tn), BF16), jax.ShapeDtypeStruct((r, ndt), F32)],
        scratch_shapes=[pltpu.VMEM((tm, d), BF16)],
        compiler_params=_params(("arbitrary", "arbitrary")),
    )(xa, nw.reshape(1, d), mod3, mod3, w_tiles, w_dt)


def _na_bias_tiles(rpb):
    cols = jnp.arange(GRID_W)
    col_start = jnp.clip(cols - NA_WIN_W // 2, 0, GRID_W - NA_WIN_W)
    col_in = (cols[None, :] >= col_start[:, None]) & (cols[None, :] < col_start[:, None] + NA_WIN_W)
    dx_idx = jnp.clip(cols[None, :] - cols[:, None], -(NA_WIN_W - 1), NA_WIN_W - 1) + NA_WIN_W - 1
    onehot = (dx_idx[:, :, None] == jnp.arange(2 * NA_WIN_W - 1)[None, None, :]).astype(F32)
    table = jnp.einsum("hyx,qkx->hyqk", rpb.astype(F32), onehot, precision=lax.Precision.HIGHEST)
    table = jnp.where(col_in[None, None], table, MASK_NEG)
    bias = jnp.stack([table[:, t:t + NA_WIN_H] for t in range(NA_WIN_H)], axis=1)
    nh = rpb.shape[0]
    return jnp.transpose(bias, (0, 1, 3, 2, 4)).reshape(nh, NA_WIN_H, GRID_W, NA_WIN_H * GRID_W)


def _na_kernel(q_ref, k_ref, v_ref, kc_ref, vc_ref, b_ref, o_ref, *, rows):
    qb = pl.program_id(2)
    kc = kc_ref[...]
    vc = vc_ref[...]
    kh = NA_WIN_H

    nr = NA_ROWS_PER_STEP
    koffs, s_loc = [], []
    for rr in range(nr):
        r = qb * nr + rr
        r0 = jnp.clip(r - kh // 2, 0, rows - kh)
        t = r0 - r + NA_WIN_H - 1
        koff = pl.multiple_of(r0 * GRID_W, GRID_W)
        koffs.append(koff)
        q = q_ref[pl.ds(rr * GRID_W, GRID_W), :]
        s_loc.append(_dot_nt(q, k_ref[pl.ds(koff, kh * GRID_W), :]) + b_ref[0, t])
    s_ctx = _dot_nt(q_ref[...], kc)
    m_ctx = jnp.max(s_ctx, axis=-1, keepdims=True)
    m = [jnp.maximum(jnp.max(s_loc[rr], axis=-1, keepdims=True), m_ctx[rr * GRID_W:(rr + 1) * GRID_W])
         for rr in range(nr)]
    p_ctx = jnp.exp(s_ctx - jnp.concatenate(m, axis=0))
    l_ctx = jnp.sum(p_ctx, axis=-1, keepdims=True)
    o_ctx = _dot(p_ctx.astype(BF16), vc)
    p_loc = [jnp.exp(s_loc[rr] - m[rr]) for rr in range(nr)]
    l = [jnp.sum(p_loc[rr], axis=-1, keepdims=True) + l_ctx[rr * GRID_W:(rr + 1) * GRID_W] for rr in range(nr)]
    o_loc = [_dot(p_loc[rr].astype(BF16), v_ref[pl.ds(koffs[rr], kh * GRID_W), :]) for rr in range(nr)]
    for rr in range(nr):
        o = o_loc[rr] + o_ctx[rr * GRID_W:(rr + 1) * GRID_W]
        o_ref[pl.ds(rr * GRID_W, GRID_W), :] = (o / l[rr]).astype(o_ref.dtype)


def _na_attention(p, bias_tiles, *, batch, seq, ctx_len, heads):
    rows = seq // GRID_W
    assert rows >= NA_WIN_H and rows % NA_ROWS_PER_STEP == 0
    qrows = NA_ROWS_PER_STEP * GRID_W
    nqb = seq // qrows
    ctx_blk0 = (batch * seq) // ctx_len
    dh = NA_HEAD_DIM
    return pl.pallas_call(
        functools.partial(_na_kernel, rows=rows),
        name="na_attn",
        grid=(heads, batch, nqb),
        in_specs=[
            pl.BlockSpec((qrows, dh), lambda h, b, i: (b * nqb + i, h)),
            pl.BlockSpec((seq, dh), lambda h, b, i: (b, heads + h)),
            pl.BlockSpec((seq, dh), lambda h, b, i: (b, 2 * heads + h)),
            pl.BlockSpec((ctx_len, dh), lambda h, b, i: (ctx_blk0 + b, heads + h)),
            pl.BlockSpec((ctx_len, dh), lambda h, b, i: (ctx_blk0 + b, 2 * heads + h)),
            pl.BlockSpec((1, NA_WIN_H, GRID_W, NA_WIN_H * GRID_W), lambda h, b, i: (h, 0, 0, 0)),
        ],
        out_specs=pl.BlockSpec((qrows, dh), lambda h, b, i: (b * nqb + i, h)),
        out_shape=jax.ShapeDtypeStruct((batch * seq, heads * dh), BF16),
        compiler_params=_params(("arbitrary", "arbitrary", "arbitrary")),
    )(p, p, p, p, p, bias_tiles)


def _ctx_attn_kernel(q_ref, k_ref, v_ref, o_ref):
    s = _dot_nt(q_ref[...], k_ref[...])
    m = jnp.max(s, axis=-1, keepdims=True)
    e = jnp.exp(s - m)
    l = jnp.sum(e, axis=-1, keepdims=True)
    o_ref[...] = (_dot(e.astype(BF16), v_ref[...]) / l).astype(o_ref.dtype)


def _ctx_attention(p, *, batch, seq, ctx_len, heads):
    ctx_blk0 = (batch * seq) // ctx_len
    dh = NA_HEAD_DIM
    return pl.pallas_call(
        _ctx_attn_kernel,
        name="ctx_attn",
        grid=(batch, heads),
        in_specs=[
            pl.BlockSpec((ctx_len, dh), lambda b, h: (ctx_blk0 + b, h)),
            pl.BlockSpec((ctx_len, dh), lambda b, h: (ctx_blk0 + b, heads + h)),
            pl.BlockSpec((ctx_len, dh), lambda b, h: (ctx_blk0 + b, 2 * heads + h)),
        ],
        out_specs=pl.BlockSpec((ctx_len, dh), lambda b, h: (b, h)),
        out_shape=jax.ShapeDtypeStruct((batch * ctx_len, heads * dh), BF16),
        compiler_params=_params(("arbitrary", "arbitrary")),
    )(p, p, p)


def _conv_kernel(prev_ref, cur_ref, next_ref, w_ref, b_ref, o_ref, *, n_lat_tiles, tiles_per_seq):
    i = pl.program_id(1)
    pos = i % tiles_per_seq
    is_lat = i < n_lat_tiles
    first = jnp.logical_or(jnp.logical_not(is_lat), pos == 0)
    last = jnp.logical_or(jnp.logical_not(is_lat), pos == tiles_per_seq - 1)
    tc = cur_ref.shape[0]
    sub = CONV_SUB
    halo = CONV_HALO
    win = sub + 2 * halo
    pad = SSD_CONV // 2
    taps = [kk for kk in range(SSD_CONV) if kk != pad]
    pv = prev_ref[...]
    nv = next_ref[...]
    ext = jnp.concatenate([jnp.where(first, jnp.zeros_like(pv), pv), cur_ref[...],
                           jnp.where(last, jnp.zeros_like(nv), nv)], axis=0)
    rr = lax.broadcasted_iota(jnp.int32, (len(taps) * sub, win), 0)
    cc = lax.broadcasted_iota(jnp.int32, (len(taps) * sub, win), 1)
    a_idx = rr // sub
    off = jnp.where(a_idx < pad, a_idx, a_idx + 1) - pad
    shift = (cc == halo + rr % sub + off).astype(BF16)
    for blk in range(tc // sub):
        shifted = _dot(shift, ext[blk * sub:blk * sub + win, :])
        acc = b_ref[...] + w_ref[pl.ds(pad, 1), :] * cur_ref[pl.ds(blk * sub, sub), :].astype(F32)
        for a, kk in enumerate(taps):
            acc = acc + w_ref[pl.ds(kk, 1), :] * shifted[a * sub:(a + 1) * sub, :]
        o_ref[pl.ds(blk * sub, sub), :] = _silu(acc).astype(o_ref.dtype)


def _ssd_conv(p, conv_w, conv_b, *, col0, n_lat_rows, seq):
    r = p.shape[0]
    c = conv_w.shape[1]
    cw = next(w for w in (2048, 1024, 512, 256, 128) if col0 % w == 0 and c % w == 0)
    tc = CONV_ROWS
    hb = tc // CONV_HALO
    nhalo = r // CONV_HALO
    cb0 = col0 // cw
    assert col0 % cw == 0 and c % cw == 0
    kern = functools.partial(_conv_kernel, n_lat_tiles=n_lat_rows // tc, tiles_per_seq=seq // tc)
    return pl.pallas_call(
        kern,
        name="ssd_conv",
        grid=(c // cw, r // tc),
        in_specs=[
            pl.BlockSpec((CONV_HALO, cw), lambda j, i: (jnp.maximum(i * hb - 1, 0), cb0 + j)),
            pl.BlockSpec((tc, cw), lambda j, i: (i, cb0 + j)),
            pl.BlockSpec((CONV_HALO, cw), lambda j, i: (jnp.minimum((i + 1) * hb, nhalo - 1), cb0 + j)),
            pl.BlockSpec((SSD_CONV, cw), lambda j, i: (0, j)),
            pl.BlockSpec((1, cw), lambda j, i: (0, j)),
        ],
        out_specs=pl.BlockSpec((tc, cw), lambda j, i: (i, j)),
        out_shape=jax.ShapeDtypeStruct((r, c), BF16),
        compiler_params=_params(("arbitrary", "arbitrary")),
    )(p, p, p, conv_w, conv_b.reshape(1, c))


def _ssd_prep_kernel(dtraw_ref, dtb_ref, alog_ref, dtt_ref, cst_ref):
    t = dtraw_ref.shape[0]
    nd = dtraw_ref.shape[1]
    v = dtraw_ref[...] + dtb_ref[...]
    dtt = (jnp.maximum(v, 0.0) + jnp.log1p(jnp.exp(-jnp.abs(v)))).T
    adt = dtt * (-LOG2E * jnp.exp(alog_ref[...]))
    jj = lax.broadcasted_iota(jnp.int32, (t, t), 0)
    ii = lax.broadcasted_iota(jnp.int32, (t, t), 1)
    upto = (jj <= ii).astype(BF16)
    from_ = (jj >= ii).astype(BF16)
    cs_f = jnp.zeros((nd, t), F32)
    cs_b = jnp.zeros((nd, t), F32)
    for piece in _split3(adt):
        cs_f = cs_f + _dot(piece, upto)
        cs_b = cs_b + _dot(piece, from_)
    row = lax.broadcasted_iota(jnp.int32, (nd, t), 0)
    dtt_ref[...] = dtt
    cst_ref[...] = jnp.where(row < nd // 2, cs_f, cs_b)


def _ssd_prep(dt_raw, dt_bias, a_log):
    r, nd = dt_raw.shape
    t = SSD_CHUNK
    out = jax.ShapeDtypeStruct((r // t * nd, t), F32)
    ospec = pl.BlockSpec((nd, t), lambda i: (i, 0))
    return pl.pallas_call(
        _ssd_prep_kernel,
        name="ssd_prep",
        grid=(r // t,),
        in_specs=[pl.BlockSpec((t, nd), lambda i: (i, 0)), pl.BlockSpec((1, nd), lambda i: (0, 0)),
                  pl.BlockSpec((nd, 1), lambda i: (0, 0))],
        out_specs=[ospec, ospec],
        out_shape=[out, out],
        compiler_params=_params(("arbitrary",)),
    )(dt_raw, dt_bias.reshape(1, nd), a_log.reshape(nd, 1))


def _ssd_scan_kernel(*refs, rev, finish, ng):
    if finish:
        (x_ref, b_ref, c_ref, dtt_ref, cst_ref, yf_ref, z_ref, dskip_ref, nw_ref, o_ref, state_ref) = refs
    else:
        (x_ref, b_ref, c_ref, dtt_ref, cst_ref, o_ref, state_ref) = refs
    s = pl.program_id(2)
    t = SSD_CHUNK
    pd = SSD_HEAD_DIM
    ns = SSD_D_STATE
    gw = x_ref.shape[1] // ng
    hg = gw // pd

    @pl.when(s == 0)
    def _():
        state_ref[...] = jnp.zeros(state_ref.shape, F32)

    def spread_matrix(width):
        rr = lax.broadcasted_iota(jnp.int32, (4 * hg, hg * width), 0) % hg
        cc = lax.broadcasted_iota(jnp.int32, (4 * hg, hg * width), 1) // width
        return (rr == cc).astype(BF16)

    def spread(a, sel):
        pieces = jnp.concatenate(list(_split3(a)) + [jnp.zeros((hg, t), BF16)], axis=0)
        return _dot_tn(pieces, sel)

    sel_t = spread_matrix(t)
    sel_p = spread_matrix(pd)
    ii = lax.broadcasted_iota(jnp.int32, (t, t), 0)
    jj = lax.broadcasted_iota(jnp.int32, (t, t), 1)
    tri = (ii <= jj) if rev else (ii >= jj)
    lane = lax.broadcasted_iota(jnp.int32, (t, 2 * pd), 1)
    zero_b = jnp.zeros((t, 2 * pd), BF16)
    groups = range(ng)

    dtt = [dtt_ref[pl.ds(gi * hg, hg), :] for gi in groups]
    cst = [cst_ref[pl.ds(gi * hg, hg), :] for gi in groups]
    cs_l = [spread(cst[gi], sel_t) for gi in groups]
    ecs_e = [spread(jnp.exp2(cst[gi]), sel_p) for gi in groups]
    w_e = []
    for gi in groups:
        tot_t = cst[gi][:, 0:1] if rev else cst[gi][:, t - 1:t]
        w_e.append(spread(dtt[gi] * jnp.exp2(tot_t - cst[gi]), sel_p))
    csd = [cst[gi] - jnp.log2(dtt[gi]) for gi in groups]
    bm = [b_ref[:, gi * ns:(gi + 1) * ns] for gi in groups]
    cm = [c_ref[:, gi * ns:(gi + 1) * ns] for gi in groups]
    cb = [_dot_nt(cm[gi], bm[gi]) for gi in groups]
    state = [state_ref[gi] for gi in groups]
    y_off = [_dot(cm[gi], state[gi].astype(BF16)) * ecs_e[gi] for gi in groups]

    y = []
    for gi in groups:
        parts = []
        for hp in range(hg // 2):
            xpair = x_ref[:, gi * gw + hp * 2 * pd:gi * gw + (hp + 1) * 2 * pd]
            acc = None
            for sub in range(2):
                h = hp * 2 + sub
                decay = jnp.exp2(jnp.where(tri, cs_l[gi][:, h * t:(h + 1) * t] - csd[gi][h:h + 1, :], MASK_NEG))
                mm = (cb[gi] * decay).astype(BF16)
                keep = (lane < pd) if sub == 0 else (lane >= pd)
                part = _dot(mm, jnp.where(keep, xpair, zero_b))
                acc = part if acc is None else acc + part
            parts.append(acc)
        y.append(jnp.concatenate(parts, axis=1) + y_off[gi])

    for gi in groups:
        xf = x_ref[:, gi * gw:(gi + 1) * gw].astype(F32)
        etot = ecs_e[gi][0:1, :] if rev else ecs_e[gi][t - 1:t, :]
        state_ref[gi] = etot * state[gi] + _dot_tn(bm[gi], (xf * w_e[gi]).astype(BF16))

    for gi in groups:
        cols = slice(gi * gw, (gi + 1) * gw)
        if finish:
            yt = yf_ref[:, cols] + y[gi] + dskip_ref[:, cols] * x_ref[:, cols].astype(F32)
            gz = yt * _silu(z_ref[:, cols].astype(F32))
            gz = gz * lax.rsqrt(jnp.mean(gz * gz, axis=-1, keepdims=True) + EPS)
            o_ref[:, cols] = (gz * nw_ref[:, cols]).astype(o_ref.dtype)
        else:
            o_ref[:, cols] = y[gi]


def _ssd_scan(u, dtt, cst, *, batch, seq, ctx_len, d_inner, rev, finish=None):
    r = u.shape[0]
    t = SSD_CHUNK
    ng = SSD_GROUPS_PER_STEP
    gw = d_inner // SSD_GROUPS
    hg = gw // SSD_HEAD_DIM
    bw = ng * gw
    sw = ng * SSD_D_STATE
    assert SSD_GROUPS % ng == 0 and d_inner % sw == 0
    dir_blocks = SSD_GROUPS // ng
    dir0 = dir_blocks if rev else 0
    lat_chunks = seq // t
    ctx_chunks = ctx_len // t
    steps = lat_chunks + ctx_chunks
    ctx_blk0 = (batch * seq) // t
    bcb = d_inner // sw

    def chunk(b, s):
        if rev:
            return jnp.where(s < ctx_chunks, ctx_blk0 + b * ctx_chunks + (ctx_chunks - 1 - s),
                             b * lat_chunks + (steps - 1 - s))
        return jnp.where(s < ctx_chunks, ctx_blk0 + b * ctx_chunks + s, b * lat_chunks + (s - ctx_chunks))

    in_specs = [
        pl.BlockSpec((t, bw), lambda b, g, s: (chunk(b, s), g)),
        pl.BlockSpec((t, sw), lambda b, g, s: (chunk(b, s), bcb + g)),
        pl.BlockSpec((t, sw), lambda b, g, s: (chunk(b, s), bcb + dir_blocks + g)),
        pl.BlockSpec((ng * hg, t), lambda b, g, s: (chunk(b, s) * 2 * dir_blocks + dir0 + g, 0)),
        pl.BlockSpec((ng * hg, t), lambda b, g, s: (chunk(b, s) * 2 * dir_blocks + dir0 + g, 0)),
    ]
    args = [u, u, u, dtt, cst]
    if finish is not None:
        y_fwd, p, z_col0, d_skip_e, norm_w = finish
        zcb = z_col0 // bw
        assert z_col0 % bw == 0
        in_specs += [
            pl.BlockSpec((t, bw), lambda b, g, s: (chunk(b, s), g)),
            pl.BlockSpec((t, bw), lambda b, g, s: (chunk(b, s), zcb + g)),
            pl.BlockSpec((1, bw), lambda b, g, s: (0, g)),
            pl.BlockSpec((1, bw), lambda b, g, s: (0, g)),
        ]
        args += [y_fwd, p, d_skip_e, norm_w]
    kern = functools.partial(_ssd_scan_kernel, rev=rev, finish=finish is not None, ng=ng)
    return pl.pallas_call(
        kern,
        name="ssd_scan_bwd" if rev else "ssd_scan_fwd",
        grid=(batch, dir_blocks, steps),
        in_specs=in_specs,
        out_specs=pl.BlockSpec((t, bw), lambda b, g, s: (chunk(b, s), g)),
        out_shape=jax.ShapeDtypeStruct((r, d_inner), BF16 if finish is not None else F32),
        scratch_shapes=[pltpu.VMEM((ng, SSD_D_STATE, gw), F32)],
        compiler_params=_params(("arbitrary", "arbitrary", "arbitrary")),
    )(*args)


def _merge_kernel(a_ref, s_ref, wa_ref, ws_ref, g1_ref, g2_ref, o_ref):
    ao = _dot(a_ref[...], wa_ref[0])
    so = _dot(s_ref[...], ws_ref[0])
    o = _sigmoid(g1_ref[...].astype(F32)) * ao + _sigmoid(g2_ref[...].astype(F32)) * so
    o_ref[...] = o.astype(o_ref.dtype)


def _merge(attn, ssd, wna_tiles, wssd_tiles, p, *, gate_col0, n_rows, tm=1024):
    wa = attn.shape[1]
    wsd = ssd.shape[1]
    nj, _, tn = wna_tiles.shape
    gb = gate_col0 // tn
    return pl.pallas_call(
        _merge_kernel,
        name="branch_merge",
        grid=(n_rows // tm, nj),
        in_specs=[
            pl.BlockSpec((tm, wa), lambda i, j: (i, 0)),
            pl.BlockSpec((tm, wsd), lambda i, j: (i, 0)),
            pl.BlockSpec((1, wa, tn), lambda i, j: (j, 0, 0)),
            pl.BlockSpec((1, wsd, tn), lambda i, j: (j, 0, 0)),
            pl.BlockSpec((tm, tn), lambda i, j: (i, gb + j)),
            pl.BlockSpec((tm, tn), lambda i, j: (i, gb + nj + j)),
        ],
        out_specs=pl.BlockSpec((tm, tn), lambda i, j: (i, j)),
        out_shape=jax.ShapeDtypeStruct((n_rows, nj * tn), BF16),
        compiler_params=_params(("arbitrary", "arbitrary")),
    )(attn, ssd, wna_tiles, wssd_tiles, p, p)


def _proj_residual_kernel(m_ref, w_ref, x_ref, g_ref, o_ref):
    o_ref[...] = x_ref[...] + g_ref[0] * _dot(m_ref[...], w_ref[0])


def _proj_residual(m, w_tiles, xa, mod3, *, gate_slot, n_rows, seq, n_lat, ctx_row, tm):
    kdim = m.shape[1]
    nj, _, tn = w_tiles.shape
    mrow = functools.partial(_mod_row, n_lat_tiles=n_lat // tm, tiles_per_batch=seq // tm, ctx_row=ctx_row)
    return pl.pallas_call(
        _proj_residual_kernel,
        name="proj_residual",
        grid=(n_rows // tm, nj),
        in_specs=[
            pl.BlockSpec((tm, kdim), lambda i, j: (i, 0)),
            pl.BlockSpec((1, kdim, tn), lambda i, j: (j, 0, 0)),
            pl.BlockSpec((tm, tn), lambda i, j: (i, j)),
            pl.BlockSpec((1, 1, tn), lambda i, j: (mrow(i), 0, gate_slot * nj + j)),
        ],
        out_specs=pl.BlockSpec((tm, tn), lambda i, j: (i, j)),
        out_shape=jax.ShapeDtypeStruct((n_rows, nj * tn), F32),
        compiler_params=_params(("arbitrary", "arbitrary")),
    )(m, w_tiles, xa, mod3)


def _ffn_up_kernel(x_ref, nw_ref, sh_ref, sc_ref, wg_ref, wu_ref, o_ref, h_scr):
    @pl.when(pl.program_id(1) == 0)
    def _():
        _norm_mod_to(h_scr, x_ref, nw_ref, sc_ref, sh_ref)

    h = h_scr[...]
    gate = _dot(h, wg_ref[0])
    up = _dot(h, wu_ref[0])
    o_ref[...] = (_silu(gate) * up).astype(o_ref.dtype)


def _ffn_up(xa, nw, mod3, wgu_tiles, *, n_rows, seq, n_lat, ctx_row, tm=1024):
    d = xa.shape[1]
    nj2, _, tn = wgu_tiles.shape
    nj = nj2 // 2
    mrow = functools.partial(_mod_row, n_lat_tiles=n_lat // tm, tiles_per_batch=seq // tm, ctx_row=ctx_row)
    return pl.pallas_call(
        _ffn_up_kernel,
        name="ffn_up",
        grid=(n_rows // tm, nj),
        in_specs=[
            pl.BlockSpec((tm, d), lambda i, j: (i, 0)),
            pl.BlockSpec((1, d), lambda i, j: (0, 0)),
            pl.BlockSpec((1, 1, d), lambda i, j: (mrow(i), 0, 3)),
            pl.BlockSpec((1, 1, d), lambda i, j: (mrow(i), 0, 4)),
            pl.BlockSpec((1, d, tn), lambda i, j: (j, 0, 0)),
            pl.BlockSpec((1, d, tn), lambda i, j: (nj + j, 0, 0)),
        ],
        out_specs=pl.BlockSpec((tm, tn), lambda i, j: (i, j)),
        out_shape=jax.ShapeDtypeStruct((n_rows, nj * tn), BF16),
        scratch_shapes=[pltpu.VMEM((tm, d), BF16)],
        compiler_params=_params(("arbitrary", "arbitrary")),
    )(xa, nw.reshape(1, d), mod3, mod3, wgu_tiles, wgu_tiles)


def _final_norm_kernel(x_ref, w_ref, o_ref):
    x = x_ref[...]
    ms = jnp.mean(x * x, axis=-1, keepdims=True)
    o_ref[...] = x * lax.rsqrt(ms + EPS) * w_ref[...]


def _final_norm(xa, w, n_rows):
    d = xa.shape[1]
    return pl.pallas_call(
        _final_norm_kernel,
        name="final_norm",
        grid=(n_rows // TM,),
        in_specs=[pl.BlockSpec((TM, d), lambda i: (i, 0)), pl.BlockSpec((1, d), lambda i: (0, 0))],
        out_specs=pl.BlockSpec((TM, d), lambda i: (i, 0)),
        out_shape=jax.ShapeDtypeStruct((n_rows, d), F32),
        compiler_params=_params(("arbitrary",)),
    )(xa, w.reshape(1, d))


def kernel(x, c, ctx, c_ctx, w_ada, b_ada, norm_mix, norm_ffn, w_in, na_rpb, conv_w, conv_b, dt_bias, a_log,
           d_skip, ssd_norm, w_br_na, w_br_ssd, w_out, w_gate_up, w_down, norm_final):
    batch, seq, d = x.shape
    ctx_len = ctx.shape[1]
    depth = w_ada.shape[0]
    heads = na_rpb.shape[1]
    na_width = heads * NA_HEAD_DIM
    ssd_heads = dt_bias.shape[-1]
    d_inner = ssd_heads * SSD_HEAD_DIM
    conv_dim = conv_w.shape[-1]
    n_lat = batch * seq
    n_all = n_lat + batch * ctx_len
    assert ctx_len == CONV_ROWS and seq % TM_BIG == 0 and (batch * ctx_len) % TM_BIG == 0 and batch < 8

    rows = dict(seq=seq, n_lat=n_lat, ctx_row=batch)

    z_col0 = 3 * na_width
    xbc_col0 = z_col0 + d_inner
    dt_col0 = xbc_col0 + conv_dim
    gate_col0 = dt_col0
    ndt = 2 * ssd_heads

    cvec = jnp.zeros((8, d), F32).at[:batch].set(c).at[batch].set(c_ctx)
    mod = _ada_mod(cvec, w_ada, b_ada)

    xa = jnp.concatenate([x.reshape(n_lat, d), ctx.reshape(batch * ctx_len, d)], axis=0)

    for li in range(depth):
        last = li == depth - 1
        n_rows = n_lat if last else n_all
        mod3 = mod[li].reshape(8, 1, 6 * d)
        w_main = jnp.concatenate([w_in[li][:, :dt_col0], w_in[li][:, dt_col0 + ndt:]], axis=1)
        w_dt = w_in[li][:, dt_col0:dt_col0 + ndt].astype(BF16)

        p, dt_raw = _inproj(xa, norm_mix[li], mod3, _col_tiles(w_main, 1024), w_dt, na_width=na_width, **rows)

        attn = _na_attention(p, _na_bias_tiles(na_rpb[li]), batch=batch, seq=seq, ctx_len=ctx_len, heads=heads)
        if not last:
            attn_ctx = _ctx_attention(p, batch=batch, seq=seq, ctx_len=ctx_len, heads=heads)
            attn = jnp.concatenate([attn, attn_ctx], axis=0)

        u = _ssd_conv(p, conv_w[li], conv_b[li], col0=xbc_col0, n_lat_rows=n_lat, seq=seq)
        dtt, cst = _ssd_prep(dt_raw, dt_bias[li], a_log[li])
        scan = functools.partial(_ssd_scan, u, dtt, cst, batch=batch, seq=seq, ctx_len=ctx_len, d_inner=d_inner)
        y_fwd = scan(rev=False)
        d_skip_e = jnp.repeat(d_skip[li], SSD_HEAD_DIM).reshape(1, d_inner)
        ssd = scan(rev=True, finish=(y_fwd, p, z_col0, d_skip_e, ssd_norm[li].reshape(1, d_inner)))

        m = _merge(attn, ssd, _col_tiles(w_br_na[li], 512), _col_tiles(w_br_ssd[li], 512), p, gate_col0=gate_col0,
                   n_rows=n_rows)
        xa = _proj_residual(m, _col_tiles(w_out[li], 512), xa, mod3, gate_slot=2, n_rows=n_rows, tm=TM_BIG, **rows)
        hmid = _ffn_up(xa, norm_ffn[li], mod3, _col_tiles(w_gate_up[li], 512), n_rows=n_rows, **rows)
        xa = _proj_residual(hmid, _col_tiles(w_down[li], 512), xa, mod3, gate_slot=5, n_rows=n_rows, tm=TM, **rows)

    return _final_norm(xa, norm_final, n_lat).reshape(batch, seq, d)
```

```python
import functools

import jax
import jax.numpy as jnp
from jax import lax
from jax.experimental import pallas as pl
from jax.experimental.pallas import tpu as pltpu

F32 = jnp.float32
BF16 = jnp.bfloat16

EPS = 1e-6
GRID_W = 64
NA_HEAD_DIM = 128
NA_WIN_H = 8
NA_WIN_W = 16
SSD_HEAD_DIM = 64
SSD_GROUPS = 8
SSD_D_STATE = 128
SSD_CONV = 5
SSD_CHUNK = 128
MASK_NEG = -1e30
LOG2E = 1.4426950408889634

TM = 512
TM_BIG = 1024
NORM_SLAB = 256
RETILE_COLS = 2048
RETILE_ROWS = 512
NA_ROWS_PER_STEP = 16
CONV_ROWS = 256
CONV_HALO = 16
CONV_SUB = 128
SSD_GROUPS_PER_STEP = 4
PREP_CHUNKS_PER_STEP = 4
VMEM_LIMIT = 56 * 1024 * 1024


def _params(sem):
    return pltpu.CompilerParams(dimension_semantics=sem, vmem_limit_bytes=VMEM_LIMIT)


def _dot(a, b):
    return jnp.dot(a, b, preferred_element_type=F32)


def _dot_nt(a, b):
    return lax.dot_general(a, b, (((1,), (1,)), ((), ())), preferred_element_type=F32)


def _dot_tn(a, b):
    return lax.dot_general(a, b, (((0,), (0,)), ((), ())), preferred_element_type=F32)


def _split3(a):
    p1 = a.astype(BF16)
    r1 = a - p1.astype(F32)
    p2 = r1.astype(BF16)
    r2 = r1 - p2.astype(F32)
    return p1, p2, r2.astype(BF16)


def _sigmoid(v):
    return 1.0 / (1.0 + jnp.exp(-v))


def _silu(v):
    return v * _sigmoid(v)


def _ada_kernel(c_ref, w_ref, b_ref, o_ref):
    s1, s2, _ = _split3(_silu(c_ref[...]))
    w1, w2, _ = _split3(w_ref[0])
    o_ref[0] = _dot(s1, w1) + (_dot(s1, w2) + _dot(s2, w1)) + b_ref[0]


def _ada_mod(cvec, w_ada, b_ada, tn=512):
    nl, d, n6 = w_ada.shape
    return pl.pallas_call(
        _ada_kernel,
        name="ada_mod",
        grid=(nl, n6 // tn),
        in_specs=[
            pl.BlockSpec((8, d), lambda l, j: (0, 0)),
            pl.BlockSpec((1, d, tn), lambda l, j: (l, 0, j)),
            pl.BlockSpec((1, 1, tn), lambda l, j: (l, 0, j)),
        ],
        out_specs=pl.BlockSpec((1, 8, tn), lambda l, j: (l, 0, j)),
        out_shape=jax.ShapeDtypeStruct((nl, 8, n6), F32),
        compiler_params=_params(("arbitrary", "arbitrary")),
    )(cvec, w_ada, b_ada.reshape(nl, 1, n6))


def _mod_row(i, n_lat_tiles, tiles_per_batch, ctx_row):
    return jnp.where(i < n_lat_tiles, i // tiles_per_batch, ctx_row)


def _norm_mod(x, nw, sc, sh):
    ms = jnp.mean(x * x, axis=-1, keepdims=True)
    y = x * lax.rsqrt(ms + EPS) * nw
    return y * (1.0 + sc) + sh


def _norm_mod_to(h_scr, x_ref, nw_ref, sc_ref, sh_ref):
    nw, sc, sh = nw_ref[...], sc_ref[0], sh_ref[0]

    def slab(k, carry):
        rows = pl.ds(pl.multiple_of(k * NORM_SLAB, NORM_SLAB), NORM_SLAB)
        h_scr[rows, :] = _norm_mod(x_ref[rows, :], nw, sc, sh).astype(BF16)
        return carry

    lax.fori_loop(0, x_ref.shape[0] // NORM_SLAB, slab, 0)


def _col_tiles_kernel(w_ref, o_ref):
    tn = o_ref.shape[2]
    for c in range(o_ref.shape[0]):
        o_ref[c] = w_ref[:, c * tn:(c + 1) * tn].astype(BF16)


def _col_tiles(w, tn, n_cols=None):
    k, n = w.shape
    n_cols = n if n_cols is None else n_cols
    span = next(s for s in (RETILE_COLS, RETILE_COLS // 2, RETILE_COLS // 4) if n_cols % s == 0 and s % tn == 0)
    kc = RETILE_ROWS
    assert k % kc == 0
    per = span // tn
    return pl.pallas_call(
        _col_tiles_kernel,
        name="col_tiles",
        grid=(n_cols // span, k // kc),
        in_specs=[pl.BlockSpec((kc, span), lambda j, i: (i, j))],
        out_specs=pl.BlockSpec((per, kc, tn), lambda j, i: (j, i, 0)),
        out_shape=jax.ShapeDtypeStruct((n_cols // tn, k, tn), BF16),
        compiler_params=_params(("arbitrary", "arbitrary")),
    )(w)


def _inproj_kernel(x_ref, nw_ref, sh_ref, sc_ref, wa_ref, wb_ref, wdt_ref, o_ref, dt_ref, h_scr, *,
                   a_tiles, q_tiles, q_scale):
    j = pl.program_id(1)

    @pl.when(j == 0)
    def _():
        _norm_mod_to(h_scr, x_ref, nw_ref, sc_ref, sh_ref)
        dt_ref[...] = _dot(h_scr[...], wdt_ref[...])

    @pl.when(j < a_tiles)
    def _():
        acc = _dot(h_scr[...], wa_ref[0])
        o_ref[...] = (acc * jnp.where(j < q_tiles, q_scale, 1.0)).astype(o_ref.dtype)

    @pl.when(j >= a_tiles)
    def _():
        o_ref[...] = _dot(h_scr[...], wb_ref[0]).astype(o_ref.dtype)


def _inproj(xa, nw, mod3, wa_tiles, wb_tiles, w_dt, *, seq, n_lat, ctx_row, na_width, tm=1024):
    r, d = xa.shape
    na, _, tn = wa_tiles.shape
    nb = wb_tiles.shape[0]
    ndt = w_dt.shape[1]
    mrow = functools.partial(_mod_row, n_lat_tiles=n_lat // tm, tiles_per_batch=seq // tm, ctx_row=ctx_row)
    kern = functools.partial(_inproj_kernel, a_tiles=na, q_tiles=na_width // tn, q_scale=NA_HEAD_DIM ** -0.5)
    return pl.pallas_call(
        kern,
        name="in_proj",
        grid=(r // tm, na + nb),
        in_specs=[
            pl.BlockSpec((tm, d), lambda i, j: (i, 0)),
            pl.BlockSpec((1, d), lambda i, j: (0, 0)),
            pl.BlockSpec((1, 1, d), lambda i, j: (mrow(i), 0, 0)),
            pl.BlockSpec((1, 1, d), lambda i, j: (mrow(i), 0, 1)),
            pl.BlockSpec((1, d, tn), lambda i, j: (jnp.minimum(j, na - 1), 0, 0)),
            pl.BlockSpec((1, d, tn), lambda i, j: (jnp.maximum(j - na, 0), 0, 0)),
            pl.BlockSpec((d, ndt), lambda i, j: (0, 0)),
        ],
        out_specs=[
            pl.BlockSpec((tm, tn), lambda i, j: (i, j)),
            pl.BlockSpec((tm, ndt), lambda i, j: (i, 0)),
        ],
        out_shape=[jax.ShapeDtypeStruct((r, (na + nb) * tn), BF16), jax.ShapeDtypeStruct((r, ndt), F32)],
        scratch_shapes=[pltpu.VMEM((tm, d), BF16)],
        compiler_params=_params(("arbitrary", "arbitrary")),
    )(xa, nw.reshape(1, d), mod3, mod3, wa_tiles, wb_tiles, w_dt)


def _na_bias_tiles(rpb):
    cols = jnp.arange(GRID_W)
    col_start = jnp.clip(cols - NA_WIN_W // 2, 0, GRID_W - NA_WIN_W)
    col_in = (cols[None, :] >= col_start[:, None]) & (cols[None, :] < col_start[:, None] + NA_WIN_W)
    dx_idx = jnp.clip(cols[None, :] - cols[:, None], -(NA_WIN_W - 1), NA_WIN_W - 1) + NA_WIN_W - 1
    onehot = (dx_idx[:, :, None] == jnp.arange(2 * NA_WIN_W - 1)[None, None, :]).astype(F32)
    table = jnp.einsum("hyx,qkx->hyqk", rpb.astype(F32), onehot, precision=lax.Precision.HIGHEST)
    table = jnp.where(col_in[None, None], table, MASK_NEG)
    bias = jnp.stack([table[:, t:t + NA_WIN_H] for t in range(NA_WIN_H)], axis=1)
    nh = rpb.shape[0]
    return jnp.transpose(bias, (0, 1, 3, 2, 4)).reshape(nh, NA_WIN_H, GRID_W, NA_WIN_H * GRID_W)


def _na_kernel(q_ref, k_ref, v_ref, kc_ref, vc_ref, b_ref, o_ref, *, rows):
    qb = pl.program_id(2)
    kc = kc_ref[...]
    vc = vc_ref[...]
    kh = NA_WIN_H

    nr = NA_ROWS_PER_STEP
    koffs, s_loc = [], []
    for rr in range(nr):
        r = qb * nr + rr
        r0 = jnp.clip(r - kh // 2, 0, rows - kh)
        t = r0 - r + NA_WIN_H - 1
        koff = pl.multiple_of(r0 * GRID_W, GRID_W)
        koffs.append(koff)
        q = q_ref[pl.ds(rr * GRID_W, GRID_W), :]
        s_loc.append(_dot_nt(q, k_ref[pl.ds(koff, kh * GRID_W), :]) + b_ref[0, t])
    s_ctx = _dot_nt(q_ref[...], kc)
    m_ctx = jnp.max(s_ctx, axis=-1, keepdims=True)
    m = [jnp.maximum(jnp.max(s_loc[rr], axis=-1, keepdims=True), m_ctx[rr * GRID_W:(rr + 1) * GRID_W])
         for rr in range(nr)]
    p_ctx = jnp.exp(s_ctx - jnp.concatenate(m, axis=0))
    l_ctx = jnp.sum(p_ctx, axis=-1, keepdims=True)
    o_ctx = _dot(p_ctx.astype(BF16), vc)
    p_loc = [jnp.exp(s_loc[rr] - m[rr]) for rr in range(nr)]
    l = [jnp.sum(p_loc[rr], axis=-1, keepdims=True) + l_ctx[rr * GRID_W:(rr + 1) * GRID_W] for rr in range(nr)]
    o_loc = [_dot(p_loc[rr].astype(BF16), v_ref[pl.ds(koffs[rr], kh * GRID_W), :]) for rr in range(nr)]
    for rr in range(nr):
        o = o_loc[rr] + o_ctx[rr * GRID_W:(rr + 1) * GRID_W]
        o_ref[pl.ds(rr * GRID_W, GRID_W), :] = (o / l[rr]).astype(o_ref.dtype)


def _na_attention(p, bias_tiles, *, batch, seq, ctx_len, heads):
    rows = seq // GRID_W
    assert rows >= NA_WIN_H and rows % NA_ROWS_PER_STEP == 0
    qrows = NA_ROWS_PER_STEP * GRID_W
    nqb = seq // qrows
    ctx_blk0 = (batch * seq) // ctx_len
    dh = NA_HEAD_DIM
    return pl.pallas_call(
        functools.partial(_na_kernel, rows=rows),
        name="na_attn",
        grid=(heads, batch, nqb),
        in_specs=[
            pl.BlockSpec((qrows, dh), lambda h, b, i: (b * nqb + i, h)),
            pl.BlockSpec((seq, dh), lambda h, b, i: (b, heads + h)),
            pl.BlockSpec((seq, dh), lambda h, b, i: (b, 2 * heads + h)),
            pl.BlockSpec((ctx_len, dh), lambda h, b, i: (ctx_blk0 + b, heads + h)),
            pl.BlockSpec((ctx_len, dh), lambda h, b, i: (ctx_blk0 + b, 2 * heads + h)),
            pl.BlockSpec((1, NA_WIN_H, GRID_W, NA_WIN_H * GRID_W), lambda h, b, i: (h, 0, 0, 0)),
        ],
        out_specs=pl.BlockSpec((qrows, dh), lambda h, b, i: (b * nqb + i, h)),
        out_shape=jax.ShapeDtypeStruct((batch * seq, heads * dh), BF16),
        compiler_params=_params(("arbitrary", "arbitrary", "arbitrary")),
    )(p, p, p, p, p, bias_tiles)


def _ctx_attn_kernel(q_ref, k_ref, v_ref, o_ref):
    s = _dot_nt(q_ref[...], k_ref[...])
    m = jnp.max(s, axis=-1, keepdims=True)
    e = jnp.exp(s - m)
    l = jnp.sum(e, axis=-1, keepdims=True)
    o_ref[...] = (_dot(e.astype(BF16), v_ref[...]) / l).astype(o_ref.dtype)


def _ctx_attention(p, *, batch, seq, ctx_len, heads):
    ctx_blk0 = (batch * seq) // ctx_len
    dh = NA_HEAD_DIM
    return pl.pallas_call(
        _ctx_attn_kernel,
        name="ctx_attn",
        grid=(batch, heads),
        in_specs=[
            pl.BlockSpec((ctx_len, dh), lambda b, h: (ctx_blk0 + b, h)),
            pl.BlockSpec((ctx_len, dh), lambda b, h: (ctx_blk0 + b, heads + h)),
            pl.BlockSpec((ctx_len, dh), lambda b, h: (ctx_blk0 + b, 2 * heads + h)),
        ],
        out_specs=pl.BlockSpec((ctx_len, dh), lambda b, h: (b, h)),
        out_shape=jax.ShapeDtypeStruct((batch * ctx_len, heads * dh), BF16),
        compiler_params=_params(("arbitrary", "arbitrary")),
    )(p, p, p)


def _conv_kernel(prev_ref, cur_ref, next_ref, w_ref, b_ref, o_ref, *, n_lat_tiles, tiles_per_seq):
    i = pl.program_id(1)
    pos = i % tiles_per_seq
    is_lat = i < n_lat_tiles
    first = jnp.logical_or(jnp.logical_not(is_lat), pos == 0)
    last = jnp.logical_or(jnp.logical_not(is_lat), pos == tiles_per_seq - 1)
    tc = cur_ref.shape[0]
    sub = CONV_SUB
    halo = CONV_HALO
    win = sub + 2 * halo
    pad = SSD_CONV // 2
    taps = [kk for kk in range(SSD_CONV) if kk != pad]
    pv = prev_ref[...]
    nv = next_ref[...]
    ext = jnp.concatenate([jnp.where(first, jnp.zeros_like(pv), pv), cur_ref[...],
                           jnp.where(last, jnp.zeros_like(nv), nv)], axis=0)
    rr = lax.broadcasted_iota(jnp.int32, (len(taps) * sub, win), 0)
    cc = lax.broadcasted_iota(jnp.int32, (len(taps) * sub, win), 1)
    a_idx = rr // sub
    off = jnp.where(a_idx < pad, a_idx, a_idx + 1) - pad
    shift = (cc == halo + rr % sub + off).astype(BF16)
    for blk in range(tc // sub):
        shifted = _dot(shift, ext[blk * sub:blk * sub + win, :])
        acc = b_ref[...] + w_ref[pl.ds(pad, 1), :] * cur_ref[pl.ds(blk * sub, sub), :].astype(F32)
        for a, kk in enumerate(taps):
            acc = acc + w_ref[pl.ds(kk, 1), :] * shifted[a * sub:(a + 1) * sub, :]
        o_ref[pl.ds(blk * sub, sub), :] = _silu(acc).astype(o_ref.dtype)


def _ssd_conv(p, conv_w, conv_b, *, col0, n_lat_rows, seq):
    r = p.shape[0]
    c = conv_w.shape[1]
    cw = next(w for w in (2048, 1024, 512, 256, 128) if col0 % w == 0 and c % w == 0)
    tc = CONV_ROWS
    hb = tc // CONV_HALO
    nhalo = r // CONV_HALO
    cb0 = col0 // cw
    assert col0 % cw == 0 and c % cw == 0
    kern = functools.partial(_conv_kernel, n_lat_tiles=n_lat_rows // tc, tiles_per_seq=seq // tc)
    return pl.pallas_call(
        kern,
        name="ssd_conv",
        grid=(c // cw, r // tc),
        in_specs=[
            pl.BlockSpec((CONV_HALO, cw), lambda j, i: (jnp.maximum(i * hb - 1, 0), cb0 + j)),
            pl.BlockSpec((tc, cw), lambda j, i: (i, cb0 + j)),
            pl.BlockSpec((CONV_HALO, cw), lambda j, i: (jnp.minimum((i + 1) * hb, nhalo - 1), cb0 + j)),
            pl.BlockSpec((SSD_CONV, cw), lambda j, i: (0, j)),
            pl.BlockSpec((1, cw), lambda j, i: (0, j)),
        ],
        out_specs=pl.BlockSpec((tc, cw), lambda j, i: (i, j)),
        out_shape=jax.ShapeDtypeStruct((r, c), BF16),
        compiler_params=_params(("arbitrary", "arbitrary")),
    )(p, p, p, conv_w, conv_b.reshape(1, c))


def _ssd_prep_kernel(dtraw_ref, dtb_ref, alog_ref, dtt_ref, cst_ref):
    t = SSD_CHUNK
    nd = dtraw_ref.shape[1]
    jj = lax.broadcasted_iota(jnp.int32, (t, t), 0)
    ii = lax.broadcasted_iota(jnp.int32, (t, t), 1)
    upto = (jj <= ii).astype(BF16)
    from_ = (jj >= ii).astype(BF16)
    row = lax.broadcasted_iota(jnp.int32, (nd, t), 0)
    a2 = -LOG2E * jnp.exp(alog_ref[...])
    for ck in range(dtraw_ref.shape[0] // t):
        v = dtraw_ref[pl.ds(ck * t, t), :] + dtb_ref[...]
        dtt = (jnp.maximum(v, 0.0) + jnp.log1p(jnp.exp(-jnp.abs(v)))).T
        cs_f = jnp.zeros((nd, t), F32)
        cs_b = jnp.zeros((nd, t), F32)
        for piece in _split3(dtt * a2):
            cs_f = cs_f + _dot(piece, upto)
            cs_b = cs_b + _dot(piece, from_)
        dtt_ref[pl.ds(ck * nd, nd), :] = dtt
        cst_ref[pl.ds(ck * nd, nd), :] = jnp.where(row < nd // 2, cs_f, cs_b)


def _ssd_prep(dt_raw, dt_bias, a_log):
    r, nd = dt_raw.shape
    t = SSD_CHUNK
    cps = PREP_CHUNKS_PER_STEP
    assert r % (cps * t) == 0
    out = jax.ShapeDtypeStruct((r // t * nd, t), F32)
    ospec = pl.BlockSpec((cps * nd, t), lambda i: (i, 0))
    return pl.pallas_call(
        _ssd_prep_kernel,
        name="ssd_prep",
        grid=(r // (cps * t),),
        in_specs=[pl.BlockSpec((cps * t, nd), lambda i: (i, 0)), pl.BlockSpec((1, nd), lambda i: (0, 0)),
                  pl.BlockSpec((nd, 1), lambda i: (0, 0))],
        out_specs=[ospec, ospec],
        out_shape=[out, out],
        compiler_params=_params(("arbitrary",)),
    )(dt_raw, dt_bias.reshape(1, nd), a_log.reshape(nd, 1))


def _ssd_scan_kernel(*refs, rev, finish, ng):
    if finish:
        (x_ref, b_ref, c_ref, dtt_ref, cst_ref, yf_ref, z_ref, dskip_ref, nw_ref, o_ref, state_ref) = refs
    else:
        (x_ref, b_ref, c_ref, dtt_ref, cst_ref, o_ref, state_ref) = refs
    s = pl.program_id(2)
    t = SSD_CHUNK
    pd = SSD_HEAD_DIM
    ns = SSD_D_STATE
    gw = x_ref.shape[1] // ng
    hg = gw // pd

    @pl.when(s == 0)
    def _():
        state_ref[...] = jnp.zeros(state_ref.shape, F32)

    def spread_matrix(width):
        rr = lax.broadcasted_iota(jnp.int32, (4 * hg, hg * width), 0) % hg
        cc = lax.broadcasted_iota(jnp.int32, (4 * hg, hg * width), 1) // width
        return (rr == cc).astype(BF16)

    def spread(a, sel):
        pieces = jnp.concatenate(list(_split3(a)) + [jnp.zeros((hg, t), BF16)], axis=0)
        return _dot_tn(pieces, sel)

    sel_t = spread_matrix(t)
    sel_p = spread_matrix(pd)
    ii = lax.broadcasted_iota(jnp.int32, (t, t), 0)
    jj = lax.broadcasted_iota(jnp.int32, (t, t), 1)
    tri = (ii <= jj) if rev else (ii >= jj)
    lane = lax.broadcasted_iota(jnp.int32, (t, 2 * pd), 1)
    zero_b = jnp.zeros((t, 2 * pd), BF16)
    groups = range(ng)

    dtt = [dtt_ref[pl.ds(gi * hg, hg), :] for gi in groups]
    cst = [cst_ref[pl.ds(gi * hg, hg), :] for gi in groups]
    cs_l = [spread(cst[gi], sel_t) for gi in groups]
    ecs_e = [spread(jnp.exp2(cst[gi]), sel_p) for gi in groups]
    w_e = []
    for gi in groups:
        tot_t = cst[gi][:, 0:1] if rev else cst[gi][:, t - 1:t]
        w_e.append(spread(dtt[gi] * jnp.exp2(tot_t - cst[gi]), sel_p))
    csd = [cst[gi] - jnp.log2(dtt[gi]) for gi in groups]
    bm = [b_ref[:, gi * ns:(gi + 1) * ns] for gi in groups]
    cm = [c_ref[:, gi * ns:(gi + 1) * ns] for gi in groups]
    cb = [_dot_nt(cm[gi], bm[gi]) for gi in groups]
    state = [state_ref[gi] for gi in groups]
    y_off = [_dot(cm[gi], state[gi].astype(BF16)) * ecs_e[gi] for gi in groups]

    y = []
    for gi in groups:
        parts = []
        for hp in range(hg // 2):
            xpair = x_ref[:, gi * gw + hp * 2 * pd:gi * gw + (hp + 1) * 2 * pd]
            xbd = jnp.concatenate([jnp.where(lane < pd, xpair, zero_b), jnp.where(lane >= pd, xpair, zero_b)], axis=0)
            mms = []
            for sub in range(2):
                h = hp * 2 + sub
                decay = jnp.exp2(jnp.where(tri, cs_l[gi][:, h * t:(h + 1) * t] - csd[gi][h:h + 1, :], MASK_NEG))
                mms.append((cb[gi] * decay).astype(BF16))
            parts.append(_dot(jnp.concatenate(mms, axis=1), xbd))
        y.append(jnp.concatenate(parts, axis=1) + y_off[gi])

    for gi in groups:
        xf = x_ref[:, gi * gw:(gi + 1) * gw].astype(F32)
        etot = ecs_e[gi][0:1, :] if rev else ecs_e[gi][t - 1:t, :]
        state_ref[gi] = etot * state[gi] + _dot_tn(bm[gi], (xf * w_e[gi]).astype(BF16))

    for gi in groups:
        cols = slice(gi * gw, (gi + 1) * gw)
        if finish:
            yt = yf_ref[:, cols] + y[gi] + dskip_ref[:, cols] * x_ref[:, cols].astype(F32)
            gz = yt * _silu(z_ref[:, cols].astype(F32))
            gz = gz * lax.rsqrt(jnp.mean(gz * gz, axis=-1, keepdims=True) + EPS)
            o_ref[:, cols] = (gz * nw_ref[:, cols]).astype(o_ref.dtype)
        else:
            o_ref[:, cols] = y[gi]


def _ssd_scan(u, dtt, cst, *, batch, seq, ctx_len, d_inner, rev, finish=None):
    r = u.shape[0]
    t = SSD_CHUNK
    ng = SSD_GROUPS_PER_STEP
    gw = d_inner // SSD_GROUPS
    hg = gw // SSD_HEAD_DIM
    bw = ng * gw
    sw = ng * SSD_D_STATE
    assert SSD_GROUPS % ng == 0 and d_inner % sw == 0
    dir_blocks = SSD_GROUPS // ng
    dir0 = dir_blocks if rev else 0
    lat_chunks = seq // t
    ctx_chunks = ctx_len // t
    steps = lat_chunks + ctx_chunks
    ctx_blk0 = (batch * seq) // t
    bcb = d_inner // sw

    def chunk(b, s):
        if rev:
            return jnp.where(s < ctx_chunks, ctx_blk0 + b * ctx_chunks + (ctx_chunks - 1 - s),
                             b * lat_chunks + (steps - 1 - s))
        return jnp.where(s < ctx_chunks, ctx_blk0 + b * ctx_chunks + s, b * lat_chunks + (s - ctx_chunks))

    in_specs = [
        pl.BlockSpec((t, bw), lambda b, g, s: (chunk(b, s), g)),
        pl.BlockSpec((t, sw), lambda b, g, s: (chunk(b, s), bcb + g)),
        pl.BlockSpec((t, sw), lambda b, g, s: (chunk(b, s), bcb + dir_blocks + g)),
        pl.BlockSpec((ng * hg, t), lambda b, g, s: (chunk(b, s) * 2 * dir_blocks + dir0 + g, 0)),
        pl.BlockSpec((ng * hg, t), lambda b, g, s: (chunk(b, s) * 2 * dir_blocks + dir0 + g, 0)),
    ]
    args = [u, u, u, dtt, cst]
    if finish is not None:
        y_fwd, p, z_col0, d_skip_e, norm_w = finish
        zcb = z_col0 // bw
        assert z_col0 % bw == 0
        in_specs += [
            pl.BlockSpec((t, bw), lambda b, g, s: (chunk(b, s), g)),
            pl.BlockSpec((t, bw), lambda b, g, s: (chunk(b, s), zcb + g)),
            pl.BlockSpec((1, bw), lambda b, g, s: (0, g)),
            pl.BlockSpec((1, bw), lambda b, g, s: (0, g)),
        ]
        args += [y_fwd, p, d_skip_e, norm_w]
    kern = functools.partial(_ssd_scan_kernel, rev=rev, finish=finish is not None, ng=ng)
    return pl.pallas_call(
        kern,
        name="ssd_scan_bwd" if rev else "ssd_scan_fwd",
        grid=(batch, dir_blocks, steps),
        in_specs=in_specs,
        out_specs=pl.BlockSpec((t, bw), lambda b, g, s: (chunk(b, s), g)),
        out_shape=jax.ShapeDtypeStruct((r, d_inner), BF16 if finish is not None else F32),
        scratch_shapes=[pltpu.VMEM((ng, SSD_D_STATE, gw), F32)],
        compiler_params=_params(("arbitrary", "arbitrary", "arbitrary")),
    )(*args)


def _merge_kernel(a_ref, s_ref, wa_ref, ws_ref, g1_ref, g2_ref, o_ref):
    ao = _dot(a_ref[...], wa_ref[0])
    so = _dot(s_ref[...], ws_ref[0])
    o = _sigmoid(g1_ref[...].astype(F32)) * ao + _sigmoid(g2_ref[...].astype(F32)) * so
    o_ref[...] = o.astype(o_ref.dtype)


def _merge(attn, ssd, wna_tiles, wssd_tiles, p, *, gate_col0, n_rows, tm=1024):
    wa = attn.shape[1]
    wsd = ssd.shape[1]
    nj, _, tn = wna_tiles.shape
    gb = gate_col0 // tn
    return pl.pallas_call(
        _merge_kernel,
        name="branch_merge",
        grid=(n_rows // tm, nj),
        in_specs=[
            pl.BlockSpec((tm, wa), lambda i, j: (i, 0)),
            pl.BlockSpec((tm, wsd), lambda i, j: (i, 0)),
            pl.BlockSpec((1, wa, tn), lambda i, j: (j, 0, 0)),
            pl.BlockSpec((1, wsd, tn), lambda i, j: (j, 0, 0)),
            pl.BlockSpec((tm, tn), lambda i, j: (i, gb + j)),
            pl.BlockSpec((tm, tn), lambda i, j: (i, gb + nj + j)),
        ],
        out_specs=pl.BlockSpec((tm, tn), lambda i, j: (i, j)),
        out_shape=jax.ShapeDtypeStruct((n_rows, nj * tn), BF16),
        compiler_params=_params(("arbitrary", "arbitrary")),
    )(attn, ssd, wna_tiles, wssd_tiles, p, p)


def _proj_residual_kernel(m_ref, w_ref, x_ref, g_ref, o_ref):
    o_ref[...] = x_ref[...] + g_ref[0] * _dot(m_ref[...], w_ref[0])


def _proj_residual(m, w_tiles, xa, mod3, *, gate_slot, n_rows, seq, n_lat, ctx_row, tm):
    kdim = m.shape[1]
    nj, _, tn = w_tiles.shape
    mrow = functools.partial(_mod_row, n_lat_tiles=n_lat // tm, tiles_per_batch=seq // tm, ctx_row=ctx_row)
    return pl.pallas_call(
        _proj_residual_kernel,
        name="proj_residual",
        grid=(n_rows // tm, nj),
        in_specs=[
            pl.BlockSpec((tm, kdim), lambda i, j: (i, 0)),
            pl.BlockSpec((1, kdim, tn), lambda i, j: (j, 0, 0)),
            pl.BlockSpec((tm, tn), lambda i, j: (i, j)),
            pl.BlockSpec((1, 1, tn), lambda i, j: (mrow(i), 0, gate_slot * nj + j)),
        ],
        out_specs=pl.BlockSpec((tm, tn), lambda i, j: (i, j)),
        out_shape=jax.ShapeDtypeStruct((n_rows, nj * tn), F32),
        compiler_params=_params(("arbitrary", "arbitrary")),
    )(m, w_tiles, xa, mod3)


def _ffn_up_kernel(x_ref, nw_ref, sh_ref, sc_ref, wg_ref, wu_ref, o_ref, h_scr):
    @pl.when(pl.program_id(1) == 0)
    def _():
        _norm_mod_to(h_scr, x_ref, nw_ref, sc_ref, sh_ref)

    h = h_scr[...]
    gate = _dot(h, wg_ref[0])
    up = _dot(h, wu_ref[0])
    o_ref[...] = (_silu(gate) * up).astype(o_ref.dtype)


def _ffn_up(xa, nw, mod3, wgu_tiles, *, n_rows, seq, n_lat, ctx_row, tm=1024):
    d = xa.shape[1]
    nj2, _, tn = wgu_tiles.shape
    nj = nj2 // 2
    mrow = functools.partial(_mod_row, n_lat_tiles=n_lat // tm, tiles_per_batch=seq // tm, ctx_row=ctx_row)
    return pl.pallas_call(
        _ffn_up_kernel,
        name="ffn_up",
        grid=(n_rows // tm, nj),
        in_specs=[
            pl.BlockSpec((tm, d), lambda i, j: (i, 0)),
            pl.BlockSpec((1, d), lambda i, j: (0, 0)),
            pl.BlockSpec((1, 1, d), lambda i, j: (mrow(i), 0, 3)),
            pl.BlockSpec((1, 1, d), lambda i, j: (mrow(i), 0, 4)),
            pl.BlockSpec((1, d, tn), lambda i, j: (j, 0, 0)),
            pl.BlockSpec((1, d, tn), lambda i, j: (nj + j, 0, 0)),
        ],
        out_specs=pl.BlockSpec((tm, tn), lambda i, j: (i, j)),
        out_shape=jax.ShapeDtypeStruct((n_rows, nj * tn), BF16),
        scratch_shapes=[pltpu.VMEM((tm, d), BF16)],
        compiler_params=_params(("arbitrary", "arbitrary")),
    )(xa, nw.reshape(1, d), mod3, mod3, wgu_tiles, wgu_tiles)


def _final_norm_kernel(x_ref, w_ref, o_ref):
    x = x_ref[...]
    ms = jnp.mean(x * x, axis=-1, keepdims=True)
    o_ref[...] = x * lax.rsqrt(ms + EPS) * w_ref[...]


def _final_norm(xa, w, n_rows):
    d = xa.shape[1]
    return pl.pallas_call(
        _final_norm_kernel,
        name="final_norm",
        grid=(n_rows // TM,),
        in_specs=[pl.BlockSpec((TM, d), lambda i: (i, 0)), pl.BlockSpec((1, d), lambda i: (0, 0))],
        out_specs=pl.BlockSpec((TM, d), lambda i: (i, 0)),
        out_shape=jax.ShapeDtypeStruct((n_rows, d), F32),
        compiler_params=_params(("arbitrary",)),
    )(xa, w.reshape(1, d))


def kernel(x, c, ctx, c_ctx, w_ada, b_ada, norm_mix, norm_ffn, w_in, na_rpb, conv_w, conv_b, dt_bias, a_log,
           d_skip, ssd_norm, w_br_na, w_br_ssd, w_out, w_gate_up, w_down, norm_final):
    batch, seq, d = x.shape
    ctx_len = ctx.shape[1]
    depth = w_ada.shape[0]
    heads = na_rpb.shape[1]
    na_width = heads * NA_HEAD_DIM
    ssd_heads = dt_bias.shape[-1]
    d_inner = ssd_heads * SSD_HEAD_DIM
    conv_dim = conv_w.shape[-1]
    n_lat = batch * seq
    n_all = n_lat + batch * ctx_len
    assert ctx_len == CONV_ROWS and seq % TM_BIG == 0 and (batch * ctx_len) % TM_BIG == 0 and batch < 8

    rows = dict(seq=seq, n_lat=n_lat, ctx_row=batch)

    z_col0 = 3 * na_width
    xbc_col0 = z_col0 + d_inner
    dt_col0 = xbc_col0 + conv_dim
    gate_col0 = dt_col0
    ndt = 2 * ssd_heads

    cvec = jnp.zeros((8, d), F32).at[:batch].set(c).at[batch].set(c_ctx)
    mod = _ada_mod(cvec, w_ada, b_ada)

    xa = jnp.concatenate([x.reshape(n_lat, d), ctx.reshape(batch * ctx_len, d)], axis=0)

    for li in range(depth):
        last = li == depth - 1
        n_rows = n_lat if last else n_all
        mod3 = mod[li].reshape(8, 1, 6 * d)
        w_dt = w_in[li][:, dt_col0:dt_col0 + ndt].astype(BF16)
        wa_tiles = _col_tiles(w_in[li], 1024, n_cols=dt_col0)
        wb_tiles = _col_tiles(w_in[li][:, dt_col0 + ndt:], 1024)
        p, dt_raw = _inproj(xa, norm_mix[li], mod3, wa_tiles, wb_tiles, w_dt, na_width=na_width, **rows)

        attn = _na_attention(p, _na_bias_tiles(na_rpb[li]), batch=batch, seq=seq, ctx_len=ctx_len, heads=heads)
        if not last:
            attn_ctx = _ctx_attention(p, batch=batch, seq=seq, ctx_len=ctx_len, heads=heads)
            attn = jnp.concatenate([attn, attn_ctx], axis=0)

        u = _ssd_conv(p, conv_w[li], conv_b[li], col0=xbc_col0, n_lat_rows=n_lat, seq=seq)
        dtt, cst = _ssd_prep(dt_raw, dt_bias[li], a_log[li])
        scan = functools.partial(_ssd_scan, u, dtt, cst, batch=batch, seq=seq, ctx_len=ctx_len, d_inner=d_inner)
        y_fwd = scan(rev=False)
        d_skip_e = jnp.repeat(d_skip[li], SSD_HEAD_DIM).reshape(1, d_inner)
        ssd = scan(rev=True, finish=(y_fwd, p, z_col0, d_skip_e, ssd_norm[li].reshape(1, d_inner)))

        m = _merge(attn, ssd, _col_tiles(w_br_na[li], 512), _col_tiles(w_br_ssd[li], 512), p, gate_col0=gate_col0,
                   n_rows=n_rows)
        xa = _proj_residual(m, _col_tiles(w_out[li], 512), xa, mod3, gate_slot=2, n_rows=n_rows, tm=TM_BIG, **rows)
        hmid = _ffn_up(xa, norm_ffn[li], mod3, _col_tiles(w_gate_up[li], 512), n_rows=n_rows, **rows)
        xa = _proj_residual(hmid, _col_tiles(w_down[li], 512), xa, mod3, gate_slot=5, n_rows=n_rows, tm=TM, **rows)

    return _final_norm(xa, norm_final, n_lat).reshape(batch, seq, d)
```

```python
import functools

import jax
import jax.numpy as jnp
from jax import lax
from jax.experimental import pallas as pl
from jax.experimental.pallas import tpu as pltpu

F32 = jnp.float32
BF16 = jnp.bfloat16

EPS = 1e-6
GRID_W = 64
NA_HEAD_DIM = 128
NA_WIN_H = 8
NA_WIN_W = 16
SSD_HEAD_DIM = 64
SSD_GROUPS = 8
SSD_D_STATE = 128
SSD_CONV = 5
SSD_CHUNK = 128
MASK_NEG = -1e30
LOG2E = 1.4426950408889634

TM = 512
TM_BIG = 1024
NORM_SLAB = 256
RETILE_COLS = 2048
RETILE_ROWS = 512
NA_ROWS_PER_STEP = 32
CONV_ROWS = 256
CONV_HALO = 16
CONV_SUB = 128
SSD_GROUPS_PER_STEP = 4
PREP_CHUNKS_PER_STEP = 4
VMEM_LIMIT = 56 * 1024 * 1024


def _params(sem):
    return pltpu.CompilerParams(dimension_semantics=sem, vmem_limit_bytes=VMEM_LIMIT)


def _dot(a, b):
    return jnp.dot(a, b, preferred_element_type=F32)


def _dot_nt(a, b):
    return lax.dot_general(a, b, (((1,), (1,)), ((), ())), preferred_element_type=F32)


def _dot_tn(a, b):
    return lax.dot_general(a, b, (((0,), (0,)), ((), ())), preferred_element_type=F32)


def _split3(a):
    p1 = a.astype(BF16)
    r1 = a - p1.astype(F32)
    p2 = r1.astype(BF16)
    r2 = r1 - p2.astype(F32)
    return p1, p2, r2.astype(BF16)


def _sigmoid(v):
    return 1.0 / (1.0 + jnp.exp(-v))


def _silu(v):
    return v * _sigmoid(v)


def _ada_kernel(c_ref, w_ref, b_ref, o_ref):
    s1, s2, _ = _split3(_silu(c_ref[...]))
    w1, w2, _ = _split3(w_ref[0])
    o_ref[0] = _dot(s1, w1) + (_dot(s1, w2) + _dot(s2, w1)) + b_ref[0]


def _ada_mod(cvec, w_ada, b_ada, tn=512):
    nl, d, n6 = w_ada.shape
    return pl.pallas_call(
        _ada_kernel,
        name="ada_mod",
        grid=(nl, n6 // tn),
        in_specs=[
            pl.BlockSpec((8, d), lambda l, j: (0, 0)),
            pl.BlockSpec((1, d, tn), lambda l, j: (l, 0, j)),
            pl.BlockSpec((1, 1, tn), lambda l, j: (l, 0, j)),
        ],
        out_specs=pl.BlockSpec((1, 8, tn), lambda l, j: (l, 0, j)),
        out_shape=jax.ShapeDtypeStruct((nl, 8, n6), F32),
        compiler_params=_params(("arbitrary", "arbitrary")),
    )(cvec, w_ada, b_ada.reshape(nl, 1, n6))


def _mod_row(i, n_lat_tiles, tiles_per_batch, ctx_row):
    return jnp.where(i < n_lat_tiles, i // tiles_per_batch, ctx_row)


def _norm_mod(x, nw, sc, sh):
    ms = jnp.mean(x * x, axis=-1, keepdims=True)
    y = x * lax.rsqrt(ms + EPS) * nw
    return y * (1.0 + sc) + sh


def _norm_mod_to(h_scr, x_ref, nw_ref, sc_ref, sh_ref):
    nw, sc, sh = nw_ref[...], sc_ref[0], sh_ref[0]

    def slab(k, carry):
        rows = pl.ds(pl.multiple_of(k * NORM_SLAB, NORM_SLAB), NORM_SLAB)
        h_scr[rows, :] = _norm_mod(x_ref[rows, :], nw, sc, sh).astype(BF16)
        return carry

    lax.fori_loop(0, x_ref.shape[0] // NORM_SLAB, slab, 0)


def _col_tiles_kernel(w_ref, o_ref):
    tn = o_ref.shape[2]
    for c in range(o_ref.shape[0]):
        o_ref[c] = w_ref[0, :, c * tn:(c + 1) * tn].astype(BF16)


def _col_tiles(w_stack, layer, tn, skip=None):
    _, k, n = w_stack.shape
    n_cols = n if skip is None else n - skip[1]
    span = next(s for s in (RETILE_COLS, RETILE_COLS // 2, RETILE_COLS // 4)
                if n_cols % s == 0 and s % tn == 0 and (skip is None or skip[0] % s == 0))
    kc = RETILE_ROWS
    assert k % kc == 0
    per = span // tn

    def col_start(j):
        start = j * span
        return start if skip is None else start + jnp.where(start >= skip[0], skip[1], 0)

    return pl.pallas_call(
        _col_tiles_kernel,
        name="col_tiles",
        grid=(n_cols // span, k // kc),
        in_specs=[pl.BlockSpec((pl.Element(1), pl.Element(kc), pl.Element(span)),
                               lambda j, i: (layer, pl.multiple_of(i * kc, kc), pl.multiple_of(col_start(j), 128)))],
        out_specs=pl.BlockSpec((per, kc, tn), lambda j, i: (j, i, 0)),
        out_shape=jax.ShapeDtypeStruct((n_cols // tn, k, tn), BF16),
        compiler_params=_params(("arbitrary", "arbitrary")),
    )(w_stack)


def _inproj_kernel(x_ref, nw_ref, sh_ref, sc_ref, w_ref, wdt_ref, o_ref, dt_ref, h_scr, *, q_tiles, q_scale):
    j = pl.program_id(1)

    @pl.when(j == 0)
    def _():
        _norm_mod_to(h_scr, x_ref, nw_ref, sc_ref, sh_ref)
        dt_ref[...] = _dot(h_scr[...], wdt_ref[...])

    acc = _dot(h_scr[...], w_ref[0])
    acc = acc * jnp.where(j < q_tiles, q_scale, 1.0)
    o_ref[...] = acc.astype(o_ref.dtype)


def _inproj(xa, nw, mod3, w_tiles, w_dt, *, seq, n_lat, ctx_row, na_width, tm=1024):
    r, d = xa.shape
    nj, _, tn = w_tiles.shape
    ndt = w_dt.shape[1]
    mrow = functools.partial(_mod_row, n_lat_tiles=n_lat // tm, tiles_per_batch=seq // tm, ctx_row=ctx_row)
    kern = functools.partial(_inproj_kernel, q_tiles=na_width // tn, q_scale=NA_HEAD_DIM ** -0.5)
    return pl.pallas_call(
        kern,
        name="in_proj",
        grid=(r // tm, nj),
        in_specs=[
            pl.BlockSpec((tm, d), lambda i, j: (i, 0)),
            pl.BlockSpec((1, d), lambda i, j: (0, 0)),
            pl.BlockSpec((1, 1, d), lambda i, j: (mrow(i), 0, 0)),
            pl.BlockSpec((1, 1, d), lambda i, j: (mrow(i), 0, 1)),
            pl.BlockSpec((1, d, tn), lambda i, j: (j, 0, 0)),
            pl.BlockSpec((d, ndt), lambda i, j: (0, 0)),
        ],
        out_specs=[
            pl.BlockSpec((tm, tn), lambda i, j: (i, j)),
            pl.BlockSpec((tm, ndt), lambda i, j: (i, 0)),
        ],
        out_shape=[jax.ShapeDtypeStruct((r, nj * tn), BF16), jax.ShapeDtypeStruct((r, ndt), F32)],
        scratch_shapes=[pltpu.VMEM((tm, d), BF16)],
        compiler_params=_params(("arbitrary", "arbitrary")),
    )(xa, nw.reshape(1, d), mod3, mod3, w_tiles, w_dt)


def _na_bias_tiles(rpb):
    cols = jnp.arange(GRID_W)
    col_start = jnp.clip(cols - NA_WIN_W // 2, 0, GRID_W - NA_WIN_W)
    col_in = (cols[None, :] >= col_start[:, None]) & (cols[None, :] < col_start[:, None] + NA_WIN_W)
    dx_idx = jnp.clip(cols[None, :] - cols[:, None], -(NA_WIN_W - 1), NA_WIN_W - 1) + NA_WIN_W - 1
    onehot = (dx_idx[:, :, None] == jnp.arange(2 * NA_WIN_W - 1)[None, None, :]).astype(F32)
    table = jnp.einsum("hyx,qkx->hyqk", rpb.astype(F32), onehot, precision=lax.Precision.HIGHEST)
    table = jnp.where(col_in[None, None], table, MASK_NEG)
    bias = jnp.stack([table[:, t:t + NA_WIN_H] for t in range(NA_WIN_H)], axis=1)
    nh = rpb.shape[0]
    return jnp.transpose(bias, (0, 1, 3, 2, 4)).reshape(nh, NA_WIN_H, GRID_W, NA_WIN_H * GRID_W)


def _na_kernel(q_ref, k_ref, v_ref, kc_ref, vc_ref, b_ref, o_ref, *, rows):
    qb = pl.program_id(2)
    kc = kc_ref[...]
    vc = vc_ref[...]
    kh = NA_WIN_H

    nr = NA_ROWS_PER_STEP
    koffs, s_loc = [], []
    for rr in range(nr):
        r = qb * nr + rr
        r0 = jnp.clip(r - kh // 2, 0, rows - kh)
        t = r0 - r + NA_WIN_H - 1
        koff = pl.multiple_of(r0 * GRID_W, GRID_W)
        koffs.append(koff)
        q = q_ref[pl.ds(rr * GRID_W, GRID_W), :]
        s_loc.append(_dot_nt(q, k_ref[pl.ds(koff, kh * GRID_W), :]) + b_ref[0, t])
    s_ctx = _dot_nt(q_ref[...], kc)
    m_ctx = jnp.max(s_ctx, axis=-1, keepdims=True)
    m = [jnp.maximum(jnp.max(s_loc[rr], axis=-1, keepdims=True), m_ctx[rr * GRID_W:(rr + 1) * GRID_W])
         for rr in range(nr)]
    p_ctx = jnp.exp(s_ctx - jnp.concatenate(m, axis=0))
    l_ctx = jnp.sum(p_ctx, axis=-1, keepdims=True)
    o_ctx = _dot(p_ctx.astype(BF16), vc)
    p_loc = [jnp.exp(s_loc[rr] - m[rr]) for rr in range(nr)]
    l = [jnp.sum(p_loc[rr], axis=-1, keepdims=True) + l_ctx[rr * GRID_W:(rr + 1) * GRID_W] for rr in range(nr)]
    o_loc = [_dot(p_loc[rr].astype(BF16), v_ref[pl.ds(koffs[rr], kh * GRID_W), :]) for rr in range(nr)]
    for rr in range(nr):
        o = o_loc[rr] + o_ctx[rr * GRID_W:(rr + 1) * GRID_W]
        o_ref[pl.ds(rr * GRID_W, GRID_W), :] = (o / l[rr]).astype(o_ref.dtype)


def _na_attention(p, bias_tiles, *, batch, seq, ctx_len, heads):
    rows = seq // GRID_W
    assert rows >= NA_WIN_H and rows % NA_ROWS_PER_STEP == 0
    qrows = NA_ROWS_PER_STEP * GRID_W
    nqb = seq // qrows
    ctx_blk0 = (batch * seq) // ctx_len
    dh = NA_HEAD_DIM
    return pl.pallas_call(
        functools.partial(_na_kernel, rows=rows),
        name="na_attn",
        grid=(heads, batch, nqb),
        in_specs=[
            pl.BlockSpec((qrows, dh), lambda h, b, i: (b * nqb + i, h)),
            pl.BlockSpec((seq, dh), lambda h, b, i: (b, heads + h)),
            pl.BlockSpec((seq, dh), lambda h, b, i: (b, 2 * heads + h)),
            pl.BlockSpec((ctx_len, dh), lambda h, b, i: (ctx_blk0 + b, heads + h)),
            pl.BlockSpec((ctx_len, dh), lambda h, b, i: (ctx_blk0 + b, 2 * heads + h)),
            pl.BlockSpec((1, NA_WIN_H, GRID_W, NA_WIN_H * GRID_W), lambda h, b, i: (h, 0, 0, 0)),
        ],
        out_specs=pl.BlockSpec((qrows, dh), lambda h, b, i: (b * nqb + i, h)),
        out_shape=jax.ShapeDtypeStruct((batch * seq, heads * dh), BF16),
        compiler_params=_params(("arbitrary", "arbitrary", "arbitrary")),
    )(p, p, p, p, p, bias_tiles)


def _ctx_attn_kernel(q_ref, k_ref, v_ref, o_ref):
    s = _dot_nt(q_ref[...], k_ref[...])
    m = jnp.max(s, axis=-1, keepdims=True)
    e = jnp.exp(s - m)
    l = jnp.sum(e, axis=-1, keepdims=True)
    o_ref[...] = (_dot(e.astype(BF16), v_ref[...]) / l).astype(o_ref.dtype)


def _ctx_attention(p, *, batch, seq, ctx_len, heads):
    ctx_blk0 = (batch * seq) // ctx_len
    dh = NA_HEAD_DIM
    return pl.pallas_call(
        _ctx_attn_kernel,
        name="ctx_attn",
        grid=(batch, heads),
        in_specs=[
            pl.BlockSpec((ctx_len, dh), lambda b, h: (ctx_blk0 + b, h)),
            pl.BlockSpec((ctx_len, dh), lambda b, h: (ctx_blk0 + b, heads + h)),
            pl.BlockSpec((ctx_len, dh), lambda b, h: (ctx_blk0 + b, 2 * heads + h)),
        ],
        out_specs=pl.BlockSpec((ctx_len, dh), lambda b, h: (b, h)),
        out_shape=jax.ShapeDtypeStruct((batch * ctx_len, heads * dh), BF16),
        compiler_params=_params(("arbitrary", "arbitrary")),
    )(p, p, p)


def _conv_kernel(prev_ref, cur_ref, next_ref, w_ref, b_ref, o_ref, *, n_lat_tiles, tiles_per_seq):
    i = pl.program_id(1)
    pos = i % tiles_per_seq
    is_lat = i < n_lat_tiles
    first = jnp.logical_or(jnp.logical_not(is_lat), pos == 0)
    last = jnp.logical_or(jnp.logical_not(is_lat), pos == tiles_per_seq - 1)
    tc = cur_ref.shape[0]
    sub = CONV_SUB
    halo = CONV_HALO
    win = sub + 2 * halo
    pad = SSD_CONV // 2
    taps = [kk for kk in range(SSD_CONV) if kk != pad]
    pv = prev_ref[...]
    nv = next_ref[...]
    ext = jnp.concatenate([jnp.where(first, jnp.zeros_like(pv), pv), cur_ref[...],
                           jnp.where(last, jnp.zeros_like(nv), nv)], axis=0)
    rr = lax.broadcasted_iota(jnp.int32, (len(taps) * sub, win), 0)
    cc = lax.broadcasted_iota(jnp.int32, (len(taps) * sub, win), 1)
    a_idx = rr // sub
    off = jnp.where(a_idx < pad, a_idx, a_idx + 1) - pad
    shift = (cc == halo + rr % sub + off).astype(BF16)
    for blk in range(tc // sub):
        shifted = _dot(shift, ext[blk * sub:blk * sub + win, :])
        acc = b_ref[...] + w_ref[pl.ds(pad, 1), :] * cur_ref[pl.ds(blk * sub, sub), :].astype(F32)
        for a, kk in enumerate(taps):
            acc = acc + w_ref[pl.ds(kk, 1), :] * shifted[a * sub:(a + 1) * sub, :]
        o_ref[pl.ds(blk * sub, sub), :] = _silu(acc).astype(o_ref.dtype)


def _ssd_conv(p, conv_w, conv_b, *, col0, n_lat_rows, seq):
    r = p.shape[0]
    c = conv_w.shape[1]
    cw = next(w for w in (2048, 1024, 512, 256, 128) if col0 % w == 0 and c % w == 0)
    tc = CONV_ROWS
    hb = tc // CONV_HALO
    nhalo = r // CONV_HALO
    cb0 = col0 // cw
    assert col0 % cw == 0 and c % cw == 0
    kern = functools.partial(_conv_kernel, n_lat_tiles=n_lat_rows // tc, tiles_per_seq=seq // tc)
    return pl.pallas_call(
        kern,
        name="ssd_conv",
        grid=(c // cw, r // tc),
        in_specs=[
            pl.BlockSpec((CONV_HALO, cw), lambda j, i: (jnp.maximum(i * hb - 1, 0), cb0 + j)),
            pl.BlockSpec((tc, cw), lambda j, i: (i, cb0 + j)),
            pl.BlockSpec((CONV_HALO, cw), lambda j, i: (jnp.minimum((i + 1) * hb, nhalo - 1), cb0 + j)),
            pl.BlockSpec((SSD_CONV, cw), lambda j, i: (0, j)),
            pl.BlockSpec((1, cw), lambda j, i: (0, j)),
        ],
        out_specs=pl.BlockSpec((tc, cw), lambda j, i: (i, j)),
        out_shape=jax.ShapeDtypeStruct((r, c), BF16),
        compiler_params=_params(("arbitrary", "arbitrary")),
    )(p, p, p, conv_w, conv_b.reshape(1, c))


def _ssd_prep_kernel(dtraw_ref, dtb_ref, alog_ref, dtt_ref, cst_ref):
    t = SSD_CHUNK
    nd = dtraw_ref.shape[1]
    jj = lax.broadcasted_iota(jnp.int32, (t, t), 0)
    ii = lax.broadcasted_iota(jnp.int32, (t, t), 1)
    upto = (jj <= ii).astype(BF16)
    from_ = (jj >= ii).astype(BF16)
    row = lax.broadcasted_iota(jnp.int32, (nd, t), 0)
    a2 = -LOG2E * jnp.exp(alog_ref[...])
    for ck in range(dtraw_ref.shape[0] // t):
        v = dtraw_ref[pl.ds(ck * t, t), :] + dtb_ref[...]
        dtt = (jnp.maximum(v, 0.0) + jnp.log1p(jnp.exp(-jnp.abs(v)))).T
        cs_f = jnp.zeros((nd, t), F32)
        cs_b = jnp.zeros((nd, t), F32)
        for piece in _split3(dtt * a2):
            cs_f = cs_f + _dot(piece, upto)
            cs_b = cs_b + _dot(piece, from_)
        dtt_ref[pl.ds(ck * nd, nd), :] = dtt
        cst_ref[pl.ds(ck * nd, nd), :] = jnp.where(row < nd // 2, cs_f, cs_b)


def _ssd_prep(dt_raw, dt_bias, a_log):
    r, nd = dt_raw.shape
    t = SSD_CHUNK
    cps = PREP_CHUNKS_PER_STEP
    assert r % (cps * t) == 0
    out = jax.ShapeDtypeStruct((r // t * nd, t), F32)
    ospec = pl.BlockSpec((cps * nd, t), lambda i: (i, 0))
    return pl.pallas_call(
        _ssd_prep_kernel,
        name="ssd_prep",
        grid=(r // (cps * t),),
        in_specs=[pl.BlockSpec((cps * t, nd), lambda i: (i, 0)), pl.BlockSpec((1, nd), lambda i: (0, 0)),
                  pl.BlockSpec((nd, 1), lambda i: (0, 0))],
        out_specs=[ospec, ospec],
        out_shape=[out, out],
        compiler_params=_params(("arbitrary",)),
    )(dt_raw, dt_bias.reshape(1, nd), a_log.reshape(nd, 1))


def _ssd_scan_kernel(*refs, rev, finish, ng):
    if finish:
        (x_ref, b_ref, c_ref, dtt_ref, cst_ref, yf_ref, z_ref, dskip_ref, nw_ref, o_ref, state_ref) = refs
    else:
        (x_ref, b_ref, c_ref, dtt_ref, cst_ref, o_ref, state_ref) = refs
    s = pl.program_id(2)
    t = SSD_CHUNK
    pd = SSD_HEAD_DIM
    ns = SSD_D_STATE
    gw = x_ref.shape[1] // ng
    hg = gw // pd

    @pl.when(s == 0)
    def _():
        state_ref[...] = jnp.zeros(state_ref.shape, F32)

    def spread_matrix(width):
        rr = lax.broadcasted_iota(jnp.int32, (4 * hg, hg * width), 0) % hg
        cc = lax.broadcasted_iota(jnp.int32, (4 * hg, hg * width), 1) // width
        return (rr == cc).astype(BF16)

    def spread(a, sel):
        pieces = jnp.concatenate(list(_split3(a)) + [jnp.zeros((hg, t), BF16)], axis=0)
        return _dot_tn(pieces, sel)

    sel_t = spread_matrix(t)
    sel_p = spread_matrix(pd)
    ii = lax.broadcasted_iota(jnp.int32, (t, t), 0)
    jj = lax.broadcasted_iota(jnp.int32, (t, t), 1)
    tri = (ii <= jj) if rev else (ii >= jj)
    lane = lax.broadcasted_iota(jnp.int32, (t, 2 * pd), 1)
    zero_b = jnp.zeros((t, 2 * pd), BF16)
    groups = range(ng)

    dtt = [dtt_ref[pl.ds(gi * hg, hg), :] for gi in groups]
    cst = [cst_ref[pl.ds(gi * hg, hg), :] for gi in groups]
    cs_l = [spread(cst[gi], sel_t) for gi in groups]
    ecs_e = [spread(jnp.exp2(cst[gi]), sel_p) for gi in groups]
    w_e = []
    for gi in groups:
        tot_t = cst[gi][:, 0:1] if rev else cst[gi][:, t - 1:t]
        w_e.append(spread(dtt[gi] * jnp.exp2(tot_t - cst[gi]), sel_p))
    csd = [cst[gi] - jnp.log2(dtt[gi]) for gi in groups]
    bm = [b_ref[:, gi * ns:(gi + 1) * ns] for gi in groups]
    cm = [c_ref[:, gi * ns:(gi + 1) * ns] for gi in groups]
    cb = [_dot_nt(cm[gi], bm[gi]) for gi in groups]
    state = [state_ref[gi] for gi in groups]
    y_off = [_dot(cm[gi], state[gi].astype(BF16)) * ecs_e[gi] for gi in groups]

    y = []
    for gi in groups:
        parts = []
        for hp in range(hg // 2):
            xpair = x_ref[:, gi * gw + hp * 2 * pd:gi * gw + (hp + 1) * 2 * pd]
            xbd = jnp.concatenate([jnp.where(lane < pd, xpair, zero_b), jnp.where(lane >= pd, xpair, zero_b)], axis=0)
            mms = []
            for sub in range(2):
                h = hp * 2 + sub
                decay = jnp.exp2(jnp.where(tri, cs_l[gi][:, h * t:(h + 1) * t] - csd[gi][h:h + 1, :], MASK_NEG))
                mms.append((cb[gi] * decay).astype(BF16))
            parts.append(_dot(jnp.concatenate(mms, axis=1), xbd))
        y.append(jnp.concatenate(parts, axis=1) + y_off[gi])

    for gi in groups:
        xf = x_ref[:, gi * gw:(gi + 1) * gw].astype(F32)
        etot = ecs_e[gi][0:1, :] if rev else ecs_e[gi][t - 1:t, :]
        state_ref[gi] = etot * state[gi] + _dot_tn(bm[gi], (xf * w_e[gi]).astype(BF16))

    for gi in groups:
        cols = slice(gi * gw, (gi + 1) * gw)
        if finish:
            yt = yf_ref[:, cols] + y[gi] + dskip_ref[:, cols] * x_ref[:, cols].astype(F32)
            gz = yt * _silu(z_ref[:, cols].astype(F32))
            gz = gz * lax.rsqrt(jnp.mean(gz * gz, axis=-1, keepdims=True) + EPS)
            o_ref[:, cols] = (gz * nw_ref[:, cols]).astype(o_ref.dtype)
        else:
            o_ref[:, cols] = y[gi]


def _ssd_scan(u, dtt, cst, *, batch, seq, ctx_len, d_inner, rev, finish=None):
    r = u.shape[0]
    t = SSD_CHUNK
    ng = SSD_GROUPS_PER_STEP
    gw = d_inner // SSD_GROUPS
    hg = gw // SSD_HEAD_DIM
    bw = ng * gw
    sw = ng * SSD_D_STATE
    assert SSD_GROUPS % ng == 0 and d_inner % sw == 0
    dir_blocks = SSD_GROUPS // ng
    dir0 = dir_blocks if rev else 0
    lat_chunks = seq // t
    ctx_chunks = ctx_len // t
    steps = lat_chunks + ctx_chunks
    ctx_blk0 = (batch * seq) // t
    bcb = d_inner // sw

    def chunk(b, s):
        if rev:
            return jnp.where(s < ctx_chunks, ctx_blk0 + b * ctx_chunks + (ctx_chunks - 1 - s),
                             b * lat_chunks + (steps - 1 - s))
        return jnp.where(s < ctx_chunks, ctx_blk0 + b * ctx_chunks + s, b * lat_chunks + (s - ctx_chunks))

    in_specs = [
        pl.BlockSpec((t, bw), lambda b, g, s: (chunk(b, s), g)),
        pl.BlockSpec((t, sw), lambda b, g, s: (chunk(b, s), bcb + g)),
        pl.BlockSpec((t, sw), lambda b, g, s: (chunk(b, s), bcb + dir_blocks + g)),
        pl.BlockSpec((ng * hg, t), lambda b, g, s: (chunk(b, s) * 2 * dir_blocks + dir0 + g, 0)),
        pl.BlockSpec((ng * hg, t), lambda b, g, s: (chunk(b, s) * 2 * dir_blocks + dir0 + g, 0)),
    ]
    args = [u, u, u, dtt, cst]
    if finish is not None:
        y_fwd, p, z_col0, d_skip_e, norm_w = finish
        zcb = z_col0 // bw
        assert z_col0 % bw == 0
        in_specs += [
            pl.BlockSpec((t, bw), lambda b, g, s: (chunk(b, s), g)),
            pl.BlockSpec((t, bw), lambda b, g, s: (chunk(b, s), zcb + g)),
            pl.BlockSpec((1, bw), lambda b, g, s: (0, g)),
            pl.BlockSpec((1, bw), lambda b, g, s: (0, g)),
        ]
        args += [y_fwd, p, d_skip_e, norm_w]
    kern = functools.partial(_ssd_scan_kernel, rev=rev, finish=finish is not None, ng=ng)
    return pl.pallas_call(
        kern,
        name="ssd_scan_bwd" if rev else "ssd_scan_fwd",
        grid=(batch, dir_blocks, steps),
        in_specs=in_specs,
        out_specs=pl.BlockSpec((t, bw), lambda b, g, s: (chunk(b, s), g)),
        out_shape=jax.ShapeDtypeStruct((r, d_inner), BF16 if finish is not None else F32),
        scratch_shapes=[pltpu.VMEM((ng, SSD_D_STATE, gw), F32)],
        compiler_params=_params(("arbitrary", "arbitrary", "arbitrary")),
    )(*args)


def _merge_kernel(a_ref, s_ref, wa_ref, ws_ref, g1_ref, g2_ref, o_ref):
    ao = _dot(a_ref[...], wa_ref[0])
    so = _dot(s_ref[...], ws_ref[0])
    o = _sigmoid(g1_ref[...].astype(F32)) * ao + _sigmoid(g2_ref[...].astype(F32)) * so
    o_ref[...] = o.astype(o_ref.dtype)


def _merge(attn, ssd, wna_tiles, wssd_tiles, p, *, gate_col0, n_rows, tm=1024):
    wa = attn.shape[1]
    wsd = ssd.shape[1]
    nj, _, tn = wna_tiles.shape
    gb = gate_col0 // tn
    return pl.pallas_call(
        _merge_kernel,
        name="branch_merge",
        grid=(n_rows // tm, nj),
        in_specs=[
            pl.BlockSpec((tm, wa), lambda i, j: (i, 0)),
            pl.BlockSpec((tm, wsd), lambda i, j: (i, 0)),
            pl.BlockSpec((1, wa, tn), lambda i, j: (j, 0, 0)),
            pl.BlockSpec((1, wsd, tn), lambda i, j: (j, 0, 0)),
            pl.BlockSpec((tm, tn), lambda i, j: (i, gb + j)),
            pl.BlockSpec((tm, tn), lambda i, j: (i, gb + nj + j)),
        ],
        out_specs=pl.BlockSpec((tm, tn), lambda i, j: (i, j)),
        out_shape=jax.ShapeDtypeStruct((n_rows, nj * tn), BF16),
        compiler_params=_params(("arbitrary", "arbitrary")),
    )(attn, ssd, wna_tiles, wssd_tiles, p, p)


def _proj_residual_kernel(m_ref, w_ref, x_ref, g_ref, o_ref):
    o_ref[...] = x_ref[...] + g_ref[0] * _dot(m_ref[...], w_ref[0])


def _proj_residual(m, w_tiles, xa, mod3, *, gate_slot, n_rows, seq, n_lat, ctx_row, tm):
    kdim = m.shape[1]
    nj, _, tn = w_tiles.shape
    mrow = functools.partial(_mod_row, n_lat_tiles=n_lat // tm, tiles_per_batch=seq // tm, ctx_row=ctx_row)
    return pl.pallas_call(
        _proj_residual_kernel,
        name="proj_residual",
        grid=(n_rows // tm, nj),
        in_specs=[
            pl.BlockSpec((tm, kdim), lambda i, j: (i, 0)),
            pl.BlockSpec((1, kdim, tn), lambda i, j: (j, 0, 0)),
            pl.BlockSpec((tm, tn), lambda i, j: (i, j)),
            pl.BlockSpec((1, 1, tn), lambda i, j: (mrow(i), 0, gate_slot * nj + j)),
        ],
        out_specs=pl.BlockSpec((tm, tn), lambda i, j: (i, j)),
        out_shape=jax.ShapeDtypeStruct((n_rows, nj * tn), F32),
        compiler_params=_params(("arbitrary", "arbitrary")),
    )(m, w_tiles, xa, mod3)


def _ffn_up_kernel(x_ref, nw_ref, sh_ref, sc_ref, wg_ref, wu_ref, o_ref, h_scr):
    @pl.when(pl.program_id(1) == 0)
    def _():
        _norm_mod_to(h_scr, x_ref, nw_ref, sc_ref, sh_ref)

    h = h_scr[...]
    gate = _dot(h, wg_ref[0])
    up = _dot(h, wu_ref[0])
    o_ref[...] = (_silu(gate) * up).astype(o_ref.dtype)


def _ffn_up(xa, nw, mod3, wgu_tiles, *, n_rows, seq, n_lat, ctx_row, tm=1024):
    d = xa.shape[1]
    nj2, _, tn = wgu_tiles.shape
    nj = nj2 // 2
    mrow = functools.partial(_mod_row, n_lat_tiles=n_lat // tm, tiles_per_batch=seq // tm, ctx_row=ctx_row)
    return pl.pallas_call(
        _ffn_up_kernel,
        name="ffn_up",
        grid=(n_rows // tm, nj),
        in_specs=[
            pl.BlockSpec((tm, d), lambda i, j: (i, 0)),
            pl.BlockSpec((1, d), lambda i, j: (0, 0)),
            pl.BlockSpec((1, 1, d), lambda i, j: (mrow(i), 0, 3)),
            pl.BlockSpec((1, 1, d), lambda i, j: (mrow(i), 0, 4)),
            pl.BlockSpec((1, d, tn), lambda i, j: (j, 0, 0)),
            pl.BlockSpec((1, d, tn), lambda i, j: (nj + j, 0, 0)),
        ],
        out_specs=pl.BlockSpec((tm, tn), lambda i, j: (i, j)),
        out_shape=jax.ShapeDtypeStruct((n_rows, nj * tn), BF16),
        scratch_shapes=[pltpu.VMEM((tm, d), BF16)],
        compiler_params=_params(("arbitrary", "arbitrary")),
    )(xa, nw.reshape(1, d), mod3, mod3, wgu_tiles, wgu_tiles)


def _final_norm_kernel(x_ref, w_ref, o_ref):
    x = x_ref[...]
    ms = jnp.mean(x * x, axis=-1, keepdims=True)
    o_ref[...] = x * lax.rsqrt(ms + EPS) * w_ref[...]


def _final_norm(xa, w, n_rows):
    d = xa.shape[1]
    return pl.pallas_call(
        _final_norm_kernel,
        name="final_norm",
        grid=(n_rows // TM,),
        in_specs=[pl.BlockSpec((TM, d), lambda i: (i, 0)), pl.BlockSpec((1, d), lambda i: (0, 0))],
        out_specs=pl.BlockSpec((TM, d), lambda i: (i, 0)),
        out_shape=jax.ShapeDtypeStruct((n_rows, d), F32),
        compiler_params=_params(("arbitrary",)),
    )(xa, w.reshape(1, d))


def kernel(x, c, ctx, c_ctx, w_ada, b_ada, norm_mix, norm_ffn, w_in, na_rpb, conv_w, conv_b, dt_bias, a_log,
           d_skip, ssd_norm, w_br_na, w_br_ssd, w_out, w_gate_up, w_down, norm_final):
    batch, seq, d = x.shape
    ctx_len = ctx.shape[1]
    depth = w_ada.shape[0]
    heads = na_rpb.shape[1]
    na_width = heads * NA_HEAD_DIM
    ssd_heads = dt_bias.shape[-1]
    d_inner = ssd_heads * SSD_HEAD_DIM
    conv_dim = conv_w.shape[-1]
    n_lat = batch * seq
    n_all = n_lat + batch * ctx_len
    assert ctx_len == CONV_ROWS and seq % TM_BIG == 0 and (batch * ctx_len) % TM_BIG == 0 and batch < 8

    rows = dict(seq=seq, n_lat=n_lat, ctx_row=batch)

    z_col0 = 3 * na_width
    xbc_col0 = z_col0 + d_inner
    dt_col0 = xbc_col0 + conv_dim
    gate_col0 = dt_col0
    ndt = 2 * ssd_heads

    cvec = jnp.zeros((8, d), F32).at[:batch].set(c).at[batch].set(c_ctx)
    mod = _ada_mod(cvec, w_ada, b_ada)

    xa = jnp.concatenate([x.reshape(n_lat, d), ctx.reshape(batch * ctx_len, d)], axis=0)

    for li in range(depth):
        last = li == depth - 1
        n_rows = n_lat if last else n_all
        mod3 = mod[li].reshape(8, 1, 6 * d)
        w_dt = w_in[li][:, dt_col0:dt_col0 + ndt].astype(BF16)
        w_tiles = _col_tiles(w_in, li, 1024, skip=(dt_col0, ndt))
        p, dt_raw = _inproj(xa, norm_mix[li], mod3, w_tiles, w_dt, na_width=na_width, **rows)

        attn = _na_attention(p, _na_bias_tiles(na_rpb[li]), batch=batch, seq=seq, ctx_len=ctx_len, heads=heads)
        if not last:
            attn_ctx = _ctx_attention(p, batch=batch, seq=seq, ctx_len=ctx_len, heads=heads)
            attn = jnp.concatenate([attn, attn_ctx], axis=0)

        u = _ssd_conv(p, conv_w[li], conv_b[li], col0=xbc_col0, n_lat_rows=n_lat, seq=seq)
        dtt, cst = _ssd_prep(dt_raw, dt_bias[li], a_log[li])
        scan = functools.partial(_ssd_scan, u, dtt, cst, batch=batch, seq=seq, ctx_len=ctx_len, d_inner=d_inner)
        y_fwd = scan(rev=False)
        d_skip_e = jnp.repeat(d_skip[li], SSD_HEAD_DIM).reshape(1, d_inner)
        ssd = scan(rev=True, finish=(y_fwd, p, z_col0, d_skip_e, ssd_norm[li].reshape(1, d_inner)))

        m = _merge(attn, ssd, _col_tiles(w_br_na, li, 512), _col_tiles(w_br_ssd, li, 512), p, gate_col0=gate_col0,
                   n_rows=n_rows)
        xa = _proj_residual(m, _col_tiles(w_out, li, 512), xa, mod3, gate_slot=2, n_rows=n_rows, tm=TM_BIG, **rows)
        hmid = _ffn_up(xa, norm_ffn[li], mod3, _col_tiles(w_gate_up, li, 512), n_rows=n_rows, **rows)
        xa = _proj_residual(hmid, _col_tiles(w_down, li, 512), xa, mod3, gate_slot=5, n_rows=n_rows, tm=TM, **rows)

    return _final_norm(xa, norm_final, n_lat).reshape(batch, seq, d)
```

```python
import functools

import jax
import jax.numpy as jnp
from jax import lax
from jax.experimental import pallas as pl
from jax.experimental.pallas import tpu as pltpu

F32 = jnp.float32
BF16 = jnp.bfloat16

EPS = 1e-6
GRID_W = 64
NA_HEAD_DIM = 128
NA_WIN_H = 8
NA_WIN_W = 16
SSD_HEAD_DIM = 64
SSD_GROUPS = 8
SSD_D_STATE = 128
SSD_CONV = 5
SSD_CHUNK = 128
MASK_NEG = -1e30
LOG2E = 1.4426950408889634

TM = 512
TM_BIG = 1024
NORM_SLAB = 256
RETILE_COLS = 2048
RETILE_ROWS = 512
NA_ROWS_PER_STEP = 32
CONV_ROWS = 256
CONV_HALO = 16
CONV_SUB = 128
SSD_GROUPS_PER_STEP = 4
PREP_CHUNKS_PER_STEP = 4
VMEM_LIMIT = 56 * 1024 * 1024


def _params(sem):
    return pltpu.CompilerParams(dimension_semantics=sem, vmem_limit_bytes=VMEM_LIMIT)


def _dot(a, b):
    return jnp.dot(a, b, preferred_element_type=F32)


def _dot_nt(a, b):
    return lax.dot_general(a, b, (((1,), (1,)), ((), ())), preferred_element_type=F32)


def _dot_tn(a, b):
    return lax.dot_general(a, b, (((0,), (0,)), ((), ())), preferred_element_type=F32)


def _split3(a):
    p1 = a.astype(BF16)
    r1 = a - p1.astype(F32)
    p2 = r1.astype(BF16)
    r2 = r1 - p2.astype(F32)
    return p1, p2, r2.astype(BF16)


def _sigmoid(v):
    return 1.0 / (1.0 + jnp.exp(-v))


def _silu(v):
    return v * _sigmoid(v)


def _ada_kernel(c_ref, w_ref, b_ref, o_ref):
    s1, s2, _ = _split3(_silu(c_ref[...]))
    w1, w2, _ = _split3(w_ref[0])
    o_ref[0] = _dot(s1, w1) + (_dot(s1, w2) + _dot(s2, w1)) + b_ref[0]


def _ada_mod(cvec, w_ada, b_ada, tn=512):
    nl, d, n6 = w_ada.shape
    return pl.pallas_call(
        _ada_kernel,
        name="ada_mod",
        grid=(nl, n6 // tn),
        in_specs=[
            pl.BlockSpec((8, d), lambda l, j: (0, 0)),
            pl.BlockSpec((1, d, tn), lambda l, j: (l, 0, j)),
            pl.BlockSpec((1, 1, tn), lambda l, j: (l, 0, j)),
        ],
        out_specs=pl.BlockSpec((1, 8, tn), lambda l, j: (l, 0, j)),
        out_shape=jax.ShapeDtypeStruct((nl, 8, n6), F32),
        compiler_params=_params(("arbitrary", "arbitrary")),
    )(cvec, w_ada, b_ada.reshape(nl, 1, n6))


def _mod_row(i, n_lat_tiles, tiles_per_batch, ctx_row):
    return jnp.where(i < n_lat_tiles, i // tiles_per_batch, ctx_row)


def _norm_mod(x, nw, sc, sh):
    ms = jnp.mean(x * x, axis=-1, keepdims=True)
    y = x * lax.rsqrt(ms + EPS) * nw
    return y * (1.0 + sc) + sh


def _norm_mod_to(h_scr, x_ref, nw_ref, sc_ref, sh_ref):
    nw, sc, sh = nw_ref[...], sc_ref[0], sh_ref[0]

    def slab(k, carry):
        rows = pl.ds(pl.multiple_of(k * NORM_SLAB, NORM_SLAB), NORM_SLAB)
        h_scr[rows, :] = _norm_mod(x_ref[rows, :], nw, sc, sh).astype(BF16)
        return carry

    lax.fori_loop(0, x_ref.shape[0] // NORM_SLAB, slab, 0)


def _col_tiles_kernel(w_ref, o_ref):
    tn = o_ref.shape[2]
    for c in range(o_ref.shape[0]):
        o_ref[c] = w_ref[0, :, c * tn:(c + 1) * tn].astype(BF16)


def _col_tiles(w_stack, layer, tn, skip=None):
    _, k, n = w_stack.shape
    n_cols = n if skip is None else n - skip[1]
    span = next(s for s in (RETILE_COLS, RETILE_COLS // 2, RETILE_COLS // 4)
                if n_cols % s == 0 and s % tn == 0 and (skip is None or skip[0] % s == 0))
    kc = RETILE_ROWS
    assert k % kc == 0
    per = span // tn

    def col_start(j):
        start = j * span
        return start if skip is None else start + jnp.where(start >= skip[0], skip[1], 0)

    return pl.pallas_call(
        _col_tiles_kernel,
        name="col_tiles",
        grid=(n_cols // span, k // kc),
        in_specs=[pl.BlockSpec((pl.Element(1), pl.Element(kc), pl.Element(span)),
                               lambda j, i: (layer, pl.multiple_of(i * kc, kc), pl.multiple_of(col_start(j), 128)))],
        out_specs=pl.BlockSpec((per, kc, tn), lambda j, i: (j, i, 0)),
        out_shape=jax.ShapeDtypeStruct((n_cols // tn, k, tn), BF16),
        compiler_params=_params(("arbitrary", "arbitrary")),
    )(w_stack)


def _col_slice_kernel(w_ref, o_ref):
    o_ref[...] = w_ref[0].astype(BF16)


def _col_slice(w_stack, layer, start, width):
    _, k, _ = w_stack.shape
    return pl.pallas_call(
        _col_slice_kernel,
        name="col_slice",
        grid=(1,),
        in_specs=[pl.BlockSpec((pl.Element(1), pl.Element(k), pl.Element(width)),
                               lambda i: (layer, 0, start))],
        out_specs=pl.BlockSpec((k, width), lambda i: (0, 0)),
        out_shape=jax.ShapeDtypeStruct((k, width), BF16),
        compiler_params=_params(("arbitrary",)),
    )(w_stack)


def _inproj_kernel(x_ref, nw_ref, sh_ref, sc_ref, w_ref, wdt_ref, o_ref, dt_ref, h_scr, *, q_tiles, q_scale):
    j = pl.program_id(1)

    @pl.when(j == 0)
    def _():
        _norm_mod_to(h_scr, x_ref, nw_ref, sc_ref, sh_ref)
        dt_ref[...] = _dot(h_scr[...], wdt_ref[...])

    acc = _dot(h_scr[...], w_ref[0])
    acc = acc * jnp.where(j < q_tiles, q_scale, 1.0)
    o_ref[...] = acc.astype(o_ref.dtype)


def _inproj(xa, nw, mod3, w_tiles, w_dt, *, seq, n_lat, ctx_row, na_width, tm=1024):
    r, d = xa.shape
    nj, _, tn = w_tiles.shape
    ndt = w_dt.shape[1]
    mrow = functools.partial(_mod_row, n_lat_tiles=n_lat // tm, tiles_per_batch=seq // tm, ctx_row=ctx_row)
    kern = functools.partial(_inproj_kernel, q_tiles=na_width // tn, q_scale=NA_HEAD_DIM ** -0.5)
    return pl.pallas_call(
        kern,
        name="in_proj",
        grid=(r // tm, nj),
        in_specs=[
            pl.BlockSpec((tm, d), lambda i, j: (i, 0)),
            pl.BlockSpec((1, d), lambda i, j: (0, 0)),
            pl.BlockSpec((1, 1, d), lambda i, j: (mrow(i), 0, 0)),
            pl.BlockSpec((1, 1, d), lambda i, j: (mrow(i), 0, 1)),
            pl.BlockSpec((1, d, tn), lambda i, j: (j, 0, 0)),
            pl.BlockSpec((d, ndt), lambda i, j: (0, 0)),
        ],
        out_specs=[
            pl.BlockSpec((tm, tn), lambda i, j: (i, j)),
            pl.BlockSpec((tm, ndt), lambda i, j: (i, 0)),
        ],
        out_shape=[jax.ShapeDtypeStruct((r, nj * tn), BF16), jax.ShapeDtypeStruct((r, ndt), F32)],
        scratch_shapes=[pltpu.VMEM((tm, d), BF16)],
        compiler_params=_params(("arbitrary", "arbitrary")),
    )(xa, nw.reshape(1, d), mod3, mod3, w_tiles, w_dt)


def _na_bias_tiles(rpb):
    cols = jnp.arange(GRID_W)
    col_start = jnp.clip(cols - NA_WIN_W // 2, 0, GRID_W - NA_WIN_W)
    col_in = (cols[None, :] >= col_start[:, None]) & (cols[None, :] < col_start[:, None] + NA_WIN_W)
    dx_idx = jnp.clip(cols[None, :] - cols[:, None], -(NA_WIN_W - 1), NA_WIN_W - 1) + NA_WIN_W - 1
    onehot = (dx_idx[:, :, None] == jnp.arange(2 * NA_WIN_W - 1)[None, None, :]).astype(F32)
    table = jnp.einsum("hyx,qkx->hyqk", rpb.astype(F32), onehot, precision=lax.Precision.HIGHEST)
    table = jnp.where(col_in[None, None], table, MASK_NEG)
    bias = jnp.stack([table[:, t:t + NA_WIN_H] for t in range(NA_WIN_H)], axis=1)
    nh = rpb.shape[0]
    return jnp.transpose(bias, (0, 1, 3, 2, 4)).reshape(nh, NA_WIN_H, GRID_W, NA_WIN_H * GRID_W)


def _na_kernel(q_ref, k_ref, v_ref, kc_ref, vc_ref, b_ref, o_ref, *, rows):
    qb = pl.program_id(2)
    kc = kc_ref[...]
    vc = vc_ref[...]
    kh = NA_WIN_H

    nr = NA_ROWS_PER_STEP
    koffs, s_loc = [], []
    for rr in range(nr):
        r = qb * nr + rr
        r0 = jnp.clip(r - kh // 2, 0, rows - kh)
        t = r0 - r + NA_WIN_H - 1
        koff = pl.multiple_of(r0 * GRID_W, GRID_W)
        koffs.append(koff)
        q = q_ref[pl.ds(rr * GRID_W, GRID_W), :]
        s_loc.append(_dot_nt(q, k_ref[pl.ds(koff, kh * GRID_W), :]) + b_ref[0, t])
    s_ctx = _dot_nt(q_ref[...], kc)
    m_ctx = jnp.max(s_ctx, axis=-1, keepdims=True)
    m = [jnp.maximum(jnp.max(s_loc[rr], axis=-1, keepdims=True), m_ctx[rr * GRID_W:(rr + 1) * GRID_W])
         for rr in range(nr)]
    p_ctx = jnp.exp(s_ctx - jnp.concatenate(m, axis=0))
    l_ctx = jnp.sum(p_ctx, axis=-1, keepdims=True)
    o_ctx = _dot(p_ctx.astype(BF16), vc)
    p_loc = [jnp.exp(s_loc[rr] - m[rr]) for rr in range(nr)]
    l = [jnp.sum(p_loc[rr], axis=-1, keepdims=True) + l_ctx[rr * GRID_W:(rr + 1) * GRID_W] for rr in range(nr)]
    o_loc = [_dot(p_loc[rr].astype(BF16), v_ref[pl.ds(koffs[rr], kh * GRID_W), :]) for rr in range(nr)]
    for rr in range(nr):
        o = o_loc[rr] + o_ctx[rr * GRID_W:(rr + 1) * GRID_W]
        o_ref[pl.ds(rr * GRID_W, GRID_W), :] = (o / l[rr]).astype(o_ref.dtype)


def _na_attention(p, bias_tiles, *, batch, seq, ctx_len, heads):
    rows = seq // GRID_W
    assert rows >= NA_WIN_H and rows % NA_ROWS_PER_STEP == 0
    qrows = NA_ROWS_PER_STEP * GRID_W
    nqb = seq // qrows
    ctx_blk0 = (batch * seq) // ctx_len
    dh = NA_HEAD_DIM
    return pl.pallas_call(
        functools.partial(_na_kernel, rows=rows),
        name="na_attn",
        grid=(heads, batch, nqb),
        in_specs=[
            pl.BlockSpec((qrows, dh), lambda h, b, i: (b * nqb + i, h)),
            pl.BlockSpec((seq, dh), lambda h, b, i: (b, heads + h)),
            pl.BlockSpec((seq, dh), lambda h, b, i: (b, 2 * heads + h)),
            pl.BlockSpec((ctx_len, dh), lambda h, b, i: (ctx_blk0 + b, heads + h)),
            pl.BlockSpec((ctx_len, dh), lambda h, b, i: (ctx_blk0 + b, 2 * heads + h)),
            pl.BlockSpec((1, NA_WIN_H, GRID_W, NA_WIN_H * GRID_W), lambda h, b, i: (h, 0, 0, 0)),
        ],
        out_specs=pl.BlockSpec((qrows, dh), lambda h, b, i: (b * nqb + i, h)),
        out_shape=jax.ShapeDtypeStruct((batch * seq, heads * dh), BF16),
        compiler_params=_params(("arbitrary", "arbitrary", "arbitrary")),
    )(p, p, p, p, p, bias_tiles)


def _ctx_attn_kernel(q_ref, k_ref, v_ref, o_ref):
    s = _dot_nt(q_ref[...], k_ref[...])
    m = jnp.max(s, axis=-1, keepdims=True)
    e = jnp.exp(s - m)
    l = jnp.sum(e, axis=-1, keepdims=True)
    o_ref[...] = (_dot(e.astype(BF16), v_ref[...]) / l).astype(o_ref.dtype)


def _ctx_attention(p, *, batch, seq, ctx_len, heads):
    ctx_blk0 = (batch * seq) // ctx_len
    dh = NA_HEAD_DIM
    return pl.pallas_call(
        _ctx_attn_kernel,
        name="ctx_attn",
        grid=(batch, heads),
        in_specs=[
            pl.BlockSpec((ctx_len, dh), lambda b, h: (ctx_blk0 + b, h)),
            pl.BlockSpec((ctx_len, dh), lambda b, h: (ctx_blk0 + b, heads + h)),
            pl.BlockSpec((ctx_len, dh), lambda b, h: (ctx_blk0 + b, 2 * heads + h)),
        ],
        out_specs=pl.BlockSpec((ctx_len, dh), lambda b, h: (b, h)),
        out_shape=jax.ShapeDtypeStruct((batch * ctx_len, heads * dh), BF16),
        compiler_params=_params(("arbitrary", "arbitrary")),
    )(p, p, p)


def _conv_kernel(prev_ref, cur_ref, next_ref, w_ref, b_ref, o_ref, *, n_lat_tiles, tiles_per_seq):
    i = pl.program_id(1)
    pos = i % tiles_per_seq
    is_lat = i < n_lat_tiles
    first = jnp.logical_or(jnp.logical_not(is_lat), pos == 0)
    last = jnp.logical_or(jnp.logical_not(is_lat), pos == tiles_per_seq - 1)
    tc = cur_ref.shape[0]
    sub = CONV_SUB
    halo = CONV_HALO
    win = sub + 2 * halo
    pad = SSD_CONV // 2
    taps = [kk for kk in range(SSD_CONV) if kk != pad]
    pv = prev_ref[...]
    nv = next_ref[...]
    ext = jnp.concatenate([jnp.where(first, jnp.zeros_like(pv), pv), cur_ref[...],
                           jnp.where(last, jnp.zeros_like(nv), nv)], axis=0)
    rr = lax.broadcasted_iota(jnp.int32, (len(taps) * sub, win), 0)
    cc = lax.broadcasted_iota(jnp.int32, (len(taps) * sub, win), 1)
    a_idx = rr // sub
    off = jnp.where(a_idx < pad, a_idx, a_idx + 1) - pad
    shift = (cc == halo + rr % sub + off).astype(BF16)
    for blk in range(tc // sub):
        shifted = _dot(shift, ext[blk * sub:blk * sub + win, :])
        acc = b_ref[...] + w_ref[pl.ds(pad, 1), :] * cur_ref[pl.ds(blk * sub, sub), :].astype(F32)
        for a, kk in enumerate(taps):
            acc = acc + w_ref[pl.ds(kk, 1), :] * shifted[a * sub:(a + 1) * sub, :]
        o_ref[pl.ds(blk * sub, sub), :] = _silu(acc).astype(o_ref.dtype)


def _ssd_conv(p, conv_w, conv_b, *, col0, n_lat_rows, seq):
    r = p.shape[0]
    c = conv_w.shape[1]
    cw = next(w for w in (2048, 1024, 512, 256, 128) if col0 % w == 0 and c % w == 0)
    tc = CONV_ROWS
    hb = tc // CONV_HALO
    nhalo = r // CONV_HALO
    cb0 = col0 // cw
    assert col0 % cw == 0 and c % cw == 0
    kern = functools.partial(_conv_kernel, n_lat_tiles=n_lat_rows // tc, tiles_per_seq=seq // tc)
    return pl.pallas_call(
        kern,
        name="ssd_conv",
        grid=(c // cw, r // tc),
        in_specs=[
            pl.BlockSpec((CONV_HALO, cw), lambda j, i: (jnp.maximum(i * hb - 1, 0), cb0 + j)),
            pl.BlockSpec((tc, cw), lambda j, i: (i, cb0 + j)),
            pl.BlockSpec((CONV_HALO, cw), lambda j, i: (jnp.minimum((i + 1) * hb, nhalo - 1), cb0 + j)),
            pl.BlockSpec((SSD_CONV, cw), lambda j, i: (0, j)),
            pl.BlockSpec((1, cw), lambda j, i: (0, j)),
        ],
        out_specs=pl.BlockSpec((tc, cw), lambda j, i: (i, j)),
        out_shape=jax.ShapeDtypeStruct((r, c), BF16),
        compiler_params=_params(("arbitrary", "arbitrary")),
    )(p, p, p, conv_w, conv_b.reshape(1, c))


def _ssd_prep_kernel(dtraw_ref, dtb_ref, alog_ref, dtt_ref, cst_ref):
    t = SSD_CHUNK
    nd = dtraw_ref.shape[1]
    jj = lax.broadcasted_iota(jnp.int32, (t, t), 0)
    ii = lax.broadcasted_iota(jnp.int32, (t, t), 1)
    upto = (jj <= ii).astype(BF16)
    from_ = (jj >= ii).astype(BF16)
    row = lax.broadcasted_iota(jnp.int32, (nd, t), 0)
    a2 = -LOG2E * jnp.exp(alog_ref[...])
    for ck in range(dtraw_ref.shape[0] // t):
        v = dtraw_ref[pl.ds(ck * t, t), :] + dtb_ref[...]
        dtt = (jnp.maximum(v, 0.0) + jnp.log1p(jnp.exp(-jnp.abs(v)))).T
        cs_f = jnp.zeros((nd, t), F32)
        cs_b = jnp.zeros((nd, t), F32)
        for piece in _split3(dtt * a2):
            cs_f = cs_f + _dot(piece, upto)
            cs_b = cs_b + _dot(piece, from_)
        dtt_ref[pl.ds(ck * nd, nd), :] = dtt
        cst_ref[pl.ds(ck * nd, nd), :] = jnp.where(row < nd // 2, cs_f, cs_b)


def _ssd_prep(dt_raw, dt_bias, a_log):
    r, nd = dt_raw.shape
    t = SSD_CHUNK
    cps = PREP_CHUNKS_PER_STEP
    assert r % (cps * t) == 0
    out = jax.ShapeDtypeStruct((r // t * nd, t), F32)
    ospec = pl.BlockSpec((cps * nd, t), lambda i: (i, 0))
    return pl.pallas_call(
        _ssd_prep_kernel,
        name="ssd_prep",
        grid=(r // (cps * t),),
        in_specs=[pl.BlockSpec((cps * t, nd), lambda i: (i, 0)), pl.BlockSpec((1, nd), lambda i: (0, 0)),
                  pl.BlockSpec((nd, 1), lambda i: (0, 0))],
        out_specs=[ospec, ospec],
        out_shape=[out, out],
        compiler_params=_params(("arbitrary",)),
    )(dt_raw, dt_bias.reshape(1, nd), a_log.reshape(nd, 1))


def _ssd_scan_kernel(*refs, rev, finish, ng):
    if finish:
        (x_ref, b_ref, c_ref, dtt_ref, cst_ref, yf_ref, z_ref, dskip_ref, nw_ref, o_ref, state_ref) = refs
    else:
        (x_ref, b_ref, c_ref, dtt_ref, cst_ref, o_ref, state_ref) = refs
    s = pl.program_id(2)
    t = SSD_CHUNK
    pd = SSD_HEAD_DIM
    ns = SSD_D_STATE
    gw = x_ref.shape[1] // ng
    hg = gw // pd

    @pl.when(s == 0)
    def _():
        state_ref[...] = jnp.zeros(state_ref.shape, F32)

    def spread_matrix(width):
        rr = lax.broadcasted_iota(jnp.int32, (4 * hg, hg * width), 0) % hg
        cc = lax.broadcasted_iota(jnp.int32, (4 * hg, hg * width), 1) // width
        return (rr == cc).astype(BF16)

    def spread(a, sel):
        pieces = jnp.concatenate(list(_split3(a)) + [jnp.zeros((hg, t), BF16)], axis=0)
        return _dot_tn(pieces, sel)

    sel_t = spread_matrix(t)
    sel_p = spread_matrix(pd)
    ii = lax.broadcasted_iota(jnp.int32, (t, t), 0)
    jj = lax.broadcasted_iota(jnp.int32, (t, t), 1)
    tri = (ii <= jj) if rev else (ii >= jj)
    lane = lax.broadcasted_iota(jnp.int32, (t, 2 * pd), 1)
    zero_b = jnp.zeros((t, 2 * pd), BF16)
    groups = range(ng)

    dtt = [dtt_ref[pl.ds(gi * hg, hg), :] for gi in groups]
    cst = [cst_ref[pl.ds(gi * hg, hg), :] for gi in groups]
    cs_l = [spread(cst[gi], sel_t) for gi in groups]
    ecs_e = [spread(jnp.exp2(cst[gi]), sel_p) for gi in groups]
    w_e = []
    for gi in groups:
        tot_t = cst[gi][:, 0:1] if rev else cst[gi][:, t - 1:t]
        w_e.append(spread(dtt[gi] * jnp.exp2(tot_t - cst[gi]), sel_p))
    csd = [cst[gi] - jnp.log2(dtt[gi]) for gi in groups]
    bm = [b_ref[:, gi * ns:(gi + 1) * ns] for gi in groups]
    cm = [c_ref[:, gi * ns:(gi + 1) * ns] for gi in groups]
    cb = [_dot_nt(cm[gi], bm[gi]) for gi in groups]
    state = [state_ref[gi] for gi in groups]
    y_off = [_dot(cm[gi], state[gi].astype(BF16)) * ecs_e[gi] for gi in groups]

    y = []
    for gi in groups:
        parts = []
        for hp in range(hg // 2):
            xpair = x_ref[:, gi * gw + hp * 2 * pd:gi * gw + (hp + 1) * 2 * pd]
            xbd = jnp.concatenate([jnp.where(lane < pd, xpair, zero_b), jnp.where(lane >= pd, xpair, zero_b)], axis=0)
            mms = []
            for sub in range(2):
                h = hp * 2 + sub
                decay = jnp.exp2(jnp.where(tri, cs_l[gi][:, h * t:(h + 1) * t] - csd[gi][h:h + 1, :], MASK_NEG))
                mms.append((cb[gi] * decay).astype(BF16))
            parts.append(_dot(jnp.concatenate(mms, axis=1), xbd))
        y.append(jnp.concatenate(parts, axis=1) + y_off[gi])

    for gi in groups:
        xf = x_ref[:, gi * gw:(gi + 1) * gw].astype(F32)
        etot = ecs_e[gi][0:1, :] if rev else ecs_e[gi][t - 1:t, :]
        state_ref[gi] = etot * state[gi] + _dot_tn(bm[gi], (xf * w_e[gi]).astype(BF16))

    for gi in groups:
        cols = slice(gi * gw, (gi + 1) * gw)
        if finish:
            yt = yf_ref[:, cols] + y[gi] + dskip_ref[:, cols] * x_ref[:, cols].astype(F32)
            gz = yt * _silu(z_ref[:, cols].astype(F32))
            gz = gz * lax.rsqrt(jnp.mean(gz * gz, axis=-1, keepdims=True) + EPS)
            o_ref[:, cols] = (gz * nw_ref[:, cols]).astype(o_ref.dtype)
        else:
            o_ref[:, cols] = y[gi]


def _ssd_scan(u, dtt, cst, *, batch, seq, ctx_len, d_inner, rev, finish=None):
    r = u.shape[0]
    t = SSD_CHUNK
    ng = SSD_GROUPS_PER_STEP
    gw = d_inner // SSD_GROUPS
    hg = gw // SSD_HEAD_DIM
    bw = ng * gw
    sw = ng * SSD_D_STATE
    assert SSD_GROUPS % ng == 0 and d_inner % sw == 0
    dir_blocks = SSD_GROUPS // ng
    dir0 = dir_blocks if rev else 0
    lat_chunks = seq // t
    ctx_chunks = ctx_len // t
    steps = lat_chunks + ctx_chunks
    ctx_blk0 = (batch * seq) // t
    bcb = d_inner // sw

    def chunk(b, s):
        if rev:
            return jnp.where(s < ctx_chunks, ctx_blk0 + b * ctx_chunks + (ctx_chunks - 1 - s),
                             b * lat_chunks + (steps - 1 - s))
        return jnp.where(s < ctx_chunks, ctx_blk0 + b * ctx_chunks + s, b * lat_chunks + (s - ctx_chunks))

    in_specs = [
        pl.BlockSpec((t, bw), lambda b, g, s: (chunk(b, s), g)),
        pl.BlockSpec((t, sw), lambda b, g, s: (chunk(b, s), bcb + g)),
        pl.BlockSpec((t, sw), lambda b, g, s: (chunk(b, s), bcb + dir_blocks + g)),
        pl.BlockSpec((ng * hg, t), lambda b, g, s: (chunk(b, s) * 2 * dir_blocks + dir0 + g, 0)),
        pl.BlockSpec((ng * hg, t), lambda b, g, s: (chunk(b, s) * 2 * dir_blocks + dir0 + g, 0)),
    ]
    args = [u, u, u, dtt, cst]
    if finish is not None:
        y_fwd, p, z_col0, d_skip_e, norm_w = finish
        zcb = z_col0 // bw
        assert z_col0 % bw == 0
        in_specs += [
            pl.BlockSpec((t, bw), lambda b, g, s: (chunk(b, s), g)),
            pl.BlockSpec((t, bw), lambda b, g, s: (chunk(b, s), zcb + g)),
            pl.BlockSpec((1, bw), lambda b, g, s: (0, g)),
            pl.BlockSpec((1, bw), lambda b, g, s: (0, g)),
        ]
        args += [y_fwd, p, d_skip_e, norm_w]
    kern = functools.partial(_ssd_scan_kernel, rev=rev, finish=finish is not None, ng=ng)
    return pl.pallas_call(
        kern,
        name="ssd_scan_bwd" if rev else "ssd_scan_fwd",
        grid=(batch, dir_blocks, steps),
        in_specs=in_specs,
        out_specs=pl.BlockSpec((t, bw), lambda b, g, s: (chunk(b, s), g)),
        out_shape=jax.ShapeDtypeStruct((r, d_inner), BF16 if finish is not None else F32),
        scratch_shapes=[pltpu.VMEM((ng, SSD_D_STATE, gw), F32)],
        compiler_params=_params(("arbitrary", "arbitrary", "arbitrary")),
    )(*args)


def _merge_kernel(a_ref, s_ref, wa_ref, ws_ref, g1_ref, g2_ref, o_ref):
    ao = _dot(a_ref[...], wa_ref[0])
    so = _dot(s_ref[...], ws_ref[0])
    o = _sigmoid(g1_ref[...].astype(F32)) * ao + _sigmoid(g2_ref[...].astype(F32)) * so
    o_ref[...] = o.astype(o_ref.dtype)


def _merge(attn, ssd, wna_tiles, wssd_tiles, p, *, gate_col0, n_rows, tm=1024):
    wa = attn.shape[1]
    wsd = ssd.shape[1]
    nj, _, tn = wna_tiles.shape
    gb = gate_col0 // tn
    return pl.pallas_call(
        _merge_kernel,
        name="branch_merge",
        grid=(n_rows // tm, nj),
        in_specs=[
            pl.BlockSpec((tm, wa), lambda i, j: (i, 0)),
            pl.BlockSpec((tm, wsd), lambda i, j: (i, 0)),
            pl.BlockSpec((1, wa, tn), lambda i, j: (j, 0, 0)),
            pl.BlockSpec((1, wsd, tn), lambda i, j: (j, 0, 0)),
            pl.BlockSpec((tm, tn), lambda i, j: (i, gb + j)),
            pl.BlockSpec((tm, tn), lambda i, j: (i, gb + nj + j)),
        ],
        out_specs=pl.BlockSpec((tm, tn), lambda i, j: (i, j)),
        out_shape=jax.ShapeDtypeStruct((n_rows, nj * tn), BF16),
        compiler_params=_params(("arbitrary", "arbitrary")),
    )(attn, ssd, wna_tiles, wssd_tiles, p, p)


def _proj_residual_kernel(m_ref, w_ref, x_ref, g_ref, o_ref):
    o_ref[...] = x_ref[...] + g_ref[0] * _dot(m_ref[...], w_ref[0])


def _proj_residual(m, w_tiles, xa, mod3, *, gate_slot, n_rows, seq, n_lat, ctx_row, tm):
    kdim = m.shape[1]
    nj, _, tn = w_tiles.shape
    mrow = functools.partial(_mod_row, n_lat_tiles=n_lat // tm, tiles_per_batch=seq // tm, ctx_row=ctx_row)
    return pl.pallas_call(
        _proj_residual_kernel,
        name="proj_residual",
        grid=(n_rows // tm, nj),
        in_specs=[
            pl.BlockSpec((tm, kdim), lambda i, j: (i, 0)),
            pl.BlockSpec((1, kdim, tn), lambda i, j: (j, 0, 0)),
            pl.BlockSpec((tm, tn), lambda i, j: (i, j)),
            pl.BlockSpec((1, 1, tn), lambda i, j: (mrow(i), 0, gate_slot * nj + j)),
        ],
        out_specs=pl.BlockSpec((tm, tn), lambda i, j: (i, j)),
        out_shape=jax.ShapeDtypeStruct((n_rows, nj * tn), F32),
        compiler_params=_params(("arbitrary", "arbitrary")),
    )(m, w_tiles, xa, mod3)


def _ffn_up_kernel(x_ref, nw_ref, sh_ref, sc_ref, wg_ref, wu_ref, o_ref, h_scr):
    @pl.when(pl.program_id(1) == 0)
    def _():
        _norm_mod_to(h_scr, x_ref, nw_ref, sc_ref, sh_ref)

    h = h_scr[...]
    gate = _dot(h, wg_ref[0])
    up = _dot(h, wu_ref[0])
    o_ref[...] = (_silu(gate) * up).astype(o_ref.dtype)


def _ffn_up(xa, nw, mod3, wgu_tiles, *, n_rows, seq, n_lat, ctx_row, tm=1024):
    d = xa.shape[1]
    nj2, _, tn = wgu_tiles.shape
    nj = nj2 // 2
    mrow = functools.partial(_mod_row, n_lat_tiles=n_lat // tm, tiles_per_batch=seq // tm, ctx_row=ctx_row)
    return pl.pallas_call(
        _ffn_up_kernel,
        name="ffn_up",
        grid=(n_rows // tm, nj),
        in_specs=[
            pl.BlockSpec((tm, d), lambda i, j: (i, 0)),
            pl.BlockSpec((1, d), lambda i, j: (0, 0)),
            pl.BlockSpec((1, 1, d), lambda i, j: (mrow(i), 0, 3)),
            pl.BlockSpec((1, 1, d), lambda i, j: (mrow(i), 0, 4)),
            pl.BlockSpec((1, d, tn), lambda i, j: (j, 0, 0)),
            pl.BlockSpec((1, d, tn), lambda i, j: (nj + j, 0, 0)),
        ],
        out_specs=pl.BlockSpec((tm, tn), lambda i, j: (i, j)),
        out_shape=jax.ShapeDtypeStruct((n_rows, nj * tn), BF16),
        scratch_shapes=[pltpu.VMEM((tm, d), BF16)],
        compiler_params=_params(("arbitrary", "arbitrary")),
    )(xa, nw.reshape(1, d), mod3, mod3, wgu_tiles, wgu_tiles)


def _final_norm_kernel(x_ref, w_ref, o_ref):
    x = x_ref[...]
    ms = jnp.mean(x * x, axis=-1, keepdims=True)
    o_ref[...] = x * lax.rsqrt(ms + EPS) * w_ref[...]


def _final_norm(xa, w, n_rows):
    d = xa.shape[1]
    return pl.pallas_call(
        _final_norm_kernel,
        name="final_norm",
        grid=(n_rows // TM,),
        in_specs=[pl.BlockSpec((TM, d), lambda i: (i, 0)), pl.BlockSpec((1, d), lambda i: (0, 0))],
        out_specs=pl.BlockSpec((TM, d), lambda i: (i, 0)),
        out_shape=jax.ShapeDtypeStruct((n_rows, d), F32),
        compiler_params=_params(("arbitrary",)),
    )(xa, w.reshape(1, d))


def kernel(x, c, ctx, c_ctx, w_ada, b_ada, norm_mix, norm_ffn, w_in, na_rpb, conv_w, conv_b, dt_bias, a_log,
           d_skip, ssd_norm, w_br_na, w_br_ssd, w_out, w_gate_up, w_down, norm_final):
    batch, seq, d = x.shape
    ctx_len = ctx.shape[1]
    depth = w_ada.shape[0]
    heads = na_rpb.shape[1]
    na_width = heads * NA_HEAD_DIM
    ssd_heads = dt_bias.shape[-1]
    d_inner = ssd_heads * SSD_HEAD_DIM
    conv_dim = conv_w.shape[-1]
    n_lat = batch * seq
    n_all = n_lat + batch * ctx_len
    assert ctx_len == CONV_ROWS and seq % TM_BIG == 0 and (batch * ctx_len) % TM_BIG == 0 and batch < 8

    rows = dict(seq=seq, n_lat=n_lat, ctx_row=batch)

    z_col0 = 3 * na_width
    xbc_col0 = z_col0 + d_inner
    dt_col0 = xbc_col0 + conv_dim
    gate_col0 = dt_col0
    ndt = 2 * ssd_heads

    cvec = jnp.zeros((8, d), F32).at[:batch].set(c).at[batch].set(c_ctx)
    mod = _ada_mod(cvec, w_ada, b_ada)

    xa = jnp.concatenate([x.reshape(n_lat, d), ctx.reshape(batch * ctx_len, d)], axis=0)

    for li in range(depth):
        last = li == depth - 1
        n_rows = n_lat if last else n_all
        mod3 = mod[li].reshape(8, 1, 6 * d)
        w_dt = _col_slice(w_in, li, dt_col0, ndt)
        w_tiles = _col_tiles(w_in, li, 1024, skip=(dt_col0, ndt))
        p, dt_raw = _inproj(xa, norm_mix[li], mod3, w_tiles, w_dt, na_width=na_width, **rows)

        attn = _na_attention(p, _na_bias_tiles(na_rpb[li]), batch=batch, seq=seq, ctx_len=ctx_len, heads=heads)
        if not last:
            attn_ctx = _ctx_attention(p, batch=batch, seq=seq, ctx_len=ctx_len, heads=heads)
            attn = jnp.concatenate([attn, attn_ctx], axis=0)

        u = _ssd_conv(p, conv_w[li], conv_b[li], col0=xbc_col0, n_lat_rows=n_lat, seq=seq)
        dtt, cst = _ssd_prep(dt_raw, dt_bias[li], a_log[li])
        scan = functools.partial(_ssd_scan, u, dtt, cst, batch=batch, seq=seq, ctx_len=ctx_len, d_inner=d_inner)
        y_fwd = scan(rev=False)
        d_skip_e = jnp.repeat(d_skip[li], SSD_HEAD_DIM).reshape(1, d_inner)
        ssd = scan(rev=True, finish=(y_fwd, p, z_col0, d_skip_e, ssd_norm[li].reshape(1, d_inner)))

        m = _merge(attn, ssd, _col_tiles(w_br_na, li, 512), _col_tiles(w_br_ssd, li, 512), p, gate_col0=gate_col0,
                   n_rows=n_rows)
        xa = _proj_residual(m, _col_tiles(w_out, li, 512), xa, mod3, gate_slot=2, n_rows=n_rows, tm=TM_BIG, **rows)
        hmid = _ffn_up(xa, norm_ffn[li], mod3, _col_tiles(w_gate_up, li, 512), n_rows=n_rows, **rows)
        xa = _proj_residual(hmid, _col_tiles(w_down, li, 512), xa, mod3, gate_slot=5, n_rows=n_rows, tm=TM, **rows)

    return _final_norm(xa, norm_final, n_lat).reshape(batch, seq, d)
```
